```python
import math, functools
import jax, jax.numpy as jnp
from jax import lax
import numpy as np

D_MODEL = 2048
BATCH = 2
SEQ = 4096
DEPTH = 1
DEC_BATCH = 32
DEC_SEQ = 4
PAST_LEN = 16384
PAGE_SIZE = 128

MIX_WIDTH = D_MODEL
HG_KEY_DIM = 128
HG_VAL_DIM = 128
HG_HEADS = (MIX_WIDTH // 2) // HG_KEY_DIM
HG_WIDTH = HG_HEADS * HG_KEY_DIM
HG_VWIDTH = HG_HEADS * HG_VAL_DIM
HG_CHUNK = 64
DA_HEAD_DIM = 64
DA_HEADS = (MIX_WIDTH // 2) // (2 * DA_HEAD_DIM)
DA_QK_WIDTH = DA_HEADS * 2 * DA_HEAD_DIM
DA_V_WIDTH = DA_HEADS * 2 * DA_HEAD_DIM
ROT_DIM = DA_HEAD_DIM // 4
ROPE_THETA = 500000.0
Q_BLOCK = 128
D_FF = 11 * D_MODEL // 4
CONV_W = 3
RMS_EPS = 1e-6
SPLIT_SIZES = (HG_WIDTH, HG_WIDTH, HG_VWIDTH, HG_VWIDTH, DA_QK_WIDTH, DA_QK_WIDTH, DA_V_WIDTH, D_MODEL, D_MODEL)
N_IN = sum(SPLIT_SIZES)

kernel_name = 'hybrid_hgrn2_diffattn_convffn_step'


def _rms_norm(x, w):
    xf = x.astype(jnp.float32)
    y = xf * lax.rsqrt(jnp.mean(xf * xf, axis=-1, keepdims=True) + RMS_EPS)
    return (y * w.astype(jnp.float32)).astype(x.dtype)


def _rotary(x, pos):
    half = ROT_DIM // 2
    inv_freq = jnp.power(jnp.float32(ROPE_THETA), -jnp.arange(half, dtype=jnp.float32) * (2.0 / ROT_DIM))
    ang = pos.astype(jnp.float32)[:, None] * inv_freq[None, :]
    cos = jnp.cos(ang)[None, :, None, None, :]
    sin = jnp.sin(ang)[None, :, None, None, :]
    xr = x[..., :ROT_DIM].astype(jnp.float32)
    x1, x2 = xr[..., :half], xr[..., half:]
    rot = jnp.concatenate([x1 * cos - x2 * sin, x2 * cos + x1 * sin], axis=-1)
    return jnp.concatenate([rot.astype(x.dtype), x[..., ROT_DIM:]], axis=-1)


def _hgrn2_recurrence(q, k, logf, v, s0):
    B, L, H, K = q.shape
    V = v.shape[-1]
    C = HG_CHUNK if L % HG_CHUNK == 0 else L
    n = L // C

    def chunks(a):
        return a.reshape(B, n, C, H, a.shape[-1]).transpose(1, 0, 3, 2, 4)

    causal = jnp.tril(jnp.ones((C, C), dtype=bool))[:, :, None]

    def step(S, inp):
        qc, kc, gc, vc = inp
        b = jnp.cumsum(gc, axis=2)
        o_inter = jnp.einsum('bhtk,bhkv->bhtv', qc * jnp.exp(b), S)
        rel = jnp.where(causal, b[:, :, :, None, :] - b[:, :, None, :, :], -jnp.inf)
        a = jnp.einsum('bhtk,bhsk,bhtsk->bhts', qc, kc, jnp.exp(rel))
        o = o_inter + jnp.einsum('bhts,bhsv->bhtv', a, vc)
        b_last = b[:, :, -1, :]
        S = jnp.exp(b_last)[..., None] * S + jnp.einsum(
            'bhsk,bhsv->bhkv', kc * jnp.exp(b_last[:, :, None, :] - b), vc)
        return S, o

    S, o = lax.scan(step, s0, (chunks(q), chunks(k), chunks(logf), chunks(v)))
    o = o.transpose(1, 0, 3, 2, 4).reshape(B, L, H, V)
    return o, S


def _online_softmax_update(carry, s, vb):
    m, den, acc = carry
    m_new = jnp.maximum(m, jnp.max(s, axis=-1))
    corr = jnp.exp(m - m_new)
    pr = jnp.exp(s - m_new[..., None])
    den = den * corr + jnp.sum(pr, axis=-1)
    acc = acc * corr[..., None] + jnp.einsum('bhmqk,bkhe->bhmqe', pr, vb)
    return (m_new, den, acc)


def _diff_attn_prompt(q, k, v):
    B, L, H, _, d = q.shape
    E = v.shape[-1]
    qb_len = Q_BLOCK if L % Q_BLOCK == 0 else L
    nb = L // qb_len
    kf = k.astype(jnp.float32)
    vf = v.astype(jnp.float32)
    qblocks = (q.astype(jnp.float32) * (d ** -0.5)).reshape(B, nb, qb_len, H, 2, d).transpose(1, 0, 2, 3, 4, 5)
    kpos = jnp.arange(L)

    def one_block(args):
        qblk, bi = args
        qpos = bi * qb_len + jnp.arange(qb_len)
        s = jnp.einsum('bqhmd,bkhmd->bhmqk', qblk, kf)
        s = jnp.where(kpos[None, :] <= qpos[:, None], s, -jnp.inf)
        pr = jax.nn.softmax(s, axis=-1)
        return jnp.einsum('bhmqk,bkhe->bqhme', pr, vf)

    out = lax.map(one_block, (qblocks, jnp.arange(nb)))
    return out.transpose(1, 0, 2, 3, 4, 5).reshape(B, L, H, 2, E)


def _diff_attn_sample(q, k, v, cache_k, cache_v, page_table, layer):
    B, L, H, _, d = q.shape
    E = v.shape[-1]
    qf = q.astype(jnp.float32) * (d ** -0.5)

    def page_step(carry, pages):
        kb = cache_k[layer, pages].astype(jnp.float32).reshape(B, PAGE_SIZE, H, 2, d)
        vb = cache_v[layer, pages].astype(jnp.float32)
        s = jnp.einsum('bqhmd,bkhmd->bhmqk', qf, kb)
        return _online_softmax_update(carry, s, vb), None

    init = (jnp.full((B, H, 2, L), -jnp.inf, jnp.float32),
            jnp.zeros((B, H, 2, L), jnp.float32),
            jnp.zeros((B, H, 2, L, E), jnp.float32))
    carry, _ = lax.scan(page_step, init, page_table.T)
    s_new = jnp.einsum('bqhmd,bkhmd->bhmqk', qf, k.astype(jnp.float32))
    s_new = jnp.where(jnp.tril(jnp.ones((L, L), dtype=bool)), s_new, -jnp.inf)
    _, den, acc = _online_softmax_update(carry, s_new, v.astype(jnp.float32))
    out = acc / den[..., None]
    return out.transpose(0, 3, 1, 2, 4)


def _causal_dwconv(u, prev, w, b):
    L = u.shape[1]
    xp = jnp.concatenate([prev.astype(u.dtype), u], axis=1)
    out = b + sum(xp[:, j:j + L] * w[j] for j in range(CONV_W))
    return out, xp[:, L:]


def _layer(x, pos, attend, hg_s0, conv_prev, p, lb, layer_idx):
    B, L, _ = x.shape
    f32 = jnp.float32
    h = _rms_norm(x, p['norm_mix_pre'])
    split_at = np.cumsum(SPLIT_SIZES)[:-1].tolist()
    hq, hf, hi, hg, dq, dk, dv, ga, gb = jnp.split(h @ p['w_in'], split_at, axis=-1)
    qa = jax.nn.silu(hq.astype(f32)).reshape(B, L, HG_HEADS, HG_KEY_DIM)
    lbh = lb.astype(f32).reshape(HG_HEADS, HG_KEY_DIM)
    fa = lbh + (1.0 - lbh) * jax.nn.sigmoid(hf.astype(f32).reshape(B, L, HG_HEADS, HG_KEY_DIM))
    ia = hi.astype(f32).reshape(B, L, HG_HEADS, HG_VAL_DIM)
    oa, s_new = _hgrn2_recurrence(qa, 1.0 - fa, jnp.log(fa), ia, hg_s0.astype(f32))
    oa = _rms_norm(oa, p['hg_norm']) * jax.nn.silu(hg.astype(f32)).reshape(B, L, HG_HEADS, HG_VAL_DIM)
    oa = oa.reshape(B, L, HG_VWIDTH).astype(x.dtype)
    qb = _rotary(dq.reshape(B, L, DA_HEADS, 2, DA_HEAD_DIM), pos)
    kb = _rotary(dk.reshape(B, L, DA_HEADS, 2, DA_HEAD_DIM), pos)
    vb = dv.reshape(B, L, DA_HEADS, 2 * DA_HEAD_DIM)
    maps = attend(qb, kb, vb)
    lam_init = 0.8 - 0.6 * math.exp(-0.3 * layer_idx)
    lam = p['da_lambda'].astype(f32)
    lam_full = jnp.exp(jnp.sum(lam[0] * lam[1])) - jnp.exp(jnp.sum(lam[2] * lam[3])) + lam_init
    ob = maps[..., 0, :] - lam_full * maps[..., 1, :]
    ob = (_rms_norm(ob, p['da_subln']) * (1.0 - lam_init)).reshape(B, L, DA_V_WIDTH).astype(x.dtype)
    u = jax.nn.sigmoid(ga) * (oa @ p['w_branch_a']) + jax.nn.sigmoid(gb) * (ob @ p['w_branch_b'])
    x = x + _rms_norm(u @ p['w_out'], p['norm_mix_post'])
    h2 = _rms_norm(x, p['norm_ffn_pre'])
    gate_c, conv_new = _causal_dwconv(h2 @ p['w_ffn_gate'], conv_prev, p['conv_w'], p['conv_b'])
    ff = jax.nn.gelu(gate_c, approximate=True) * (h2 @ p['w_ffn_up'])
    x = x + _rms_norm(ff @ p['w_ffn_down'], p['norm_ffn_post'])
    k_rows = kb.reshape(B, L, DA_HEADS, 2 * DA_HEAD_DIM)
    return x, k_rows, vb, s_new, conv_new


def setup_inputs(seed: int = 0) -> dict:
    key = jax.random.key(seed)
    ks = jax.random.split(key, 24)
    f32 = jnp.float32
    n_pages = PAST_LEN // PAGE_SIZE
    n_pool = (DEC_BATCH * n_pages * 5) // 4

    def nrm(k, shape, scale):
        return jax.random.normal(k, shape, f32) * scale

    def gain(k, shape):
        return 1.0 + 0.01 * jax.random.normal(k, shape, f32)

    page_table = jax.random.permutation(ks[4], n_pool)[: DEC_BATCH * n_pages].reshape(DEC_BATCH, n_pages).astype(jnp.int32)
    return {
        'x_prompt': nrm(ks[0], (BATCH, SEQ, D_MODEL), 1.0),
        'x_sample': nrm(ks[1], (DEC_BATCH, DEC_SEQ, D_MODEL), 1.0),
        'cache_k': nrm(ks[2], (DEPTH, n_pool, PAGE_SIZE, DA_HEADS, 2 * DA_HEAD_DIM), 1.0),
        'cache_v': nrm(ks[3], (DEPTH, n_pool, PAGE_SIZE, DA_HEADS, 2 * DA_HEAD_DIM), 1.0),
        'page_table': page_table,
        'state_hgrn': nrm(ks[5], (DEPTH, DEC_BATCH, HG_HEADS, HG_KEY_DIM, HG_VAL_DIM), 0.5),
        'state_conv': nrm(ks[6], (DEPTH, DEC_BATCH, CONV_W - 1, D_FF), 1.0),
        'norm_mix_pre': gain(ks[7], (DEPTH, D_MODEL)),
        'w_in': nrm(ks[8], (DEPTH, D_MODEL, N_IN), D_MODEL ** -0.5),
        'hg_lb_logits': nrm(ks[9], (DEPTH + 1, HG_WIDTH), 0.5),
        'hg_norm': gain(ks[10], (DEPTH, HG_VAL_DIM)),
        'da_lambda': nrm(ks[11], (DEPTH, 4, DA_HEAD_DIM), 0.1),
        'da_subln': gain(ks[12], (DEPTH, 2 * DA_HEAD_DIM)),
        'w_branch_a': nrm(ks[13], (DEPTH, HG_VWIDTH, D_MODEL), HG_VWIDTH ** -0.5),
        'w_branch_b': nrm(ks[14], (DEPTH, DA_V_WIDTH, D_MODEL), DA_V_WIDTH ** -0.5),
        'w_out': nrm(ks[15], (DEPTH, D_MODEL, D_MODEL), D_MODEL ** -0.5),
        'norm_mix_post': gain(ks[16], (DEPTH, D_MODEL)),
        'norm_ffn_pre': gain(ks[17], (DEPTH, D_MODEL)),
        'w_ffn_gate': nrm(ks[18], (DEPTH, D_MODEL, D_FF), D_MODEL ** -0.5),
        'w_ffn_up': nrm(ks[19], (DEPTH, D_MODEL, D_FF), D_MODEL ** -0.5),
        'conv_w': nrm(ks[20], (DEPTH, CONV_W, D_FF), CONV_W ** -0.5),
        'conv_b': nrm(ks[21], (DEPTH, D_FF), 0.01),
        'w_ffn_down': nrm(ks[22], (DEPTH, D_FF, D_MODEL), D_FF ** -0.5),
        'norm_ffn_post': gain(ks[23], (DEPTH, D_MODEL)),
    }


def reference(x_prompt, x_sample, cache_k, cache_v, page_table, state_hgrn, state_conv,
              norm_mix_pre, w_in, hg_lb_logits, hg_norm, da_lambda, da_subln, w_branch_a, w_branch_b,
              w_out, norm_mix_post, norm_ffn_pre, w_ffn_gate, w_ffn_up, conv_w, conv_b, w_ffn_down,
              norm_ffn_post):
    lb_all = jnp.cumsum(jax.nn.softmax(hg_lb_logits.astype(jnp.float32), axis=0), axis=0)
    bp, lp = x_prompt.shape[:2]
    ls = x_sample.shape[1]
    past_len = page_table.shape[1] * PAGE_SIZE
    pos_p = jnp.arange(lp)
    pos_s = past_len + jnp.arange(ls)
    yp, ys = x_prompt, x_sample
    kp_l, vp_l, sp_l, cp_l, ks_l, vs_l, ss_l, cs_l = [], [], [], [], [], [], [], []
    for l in range(DEPTH):
        p = dict(norm_mix_pre=norm_mix_pre[l], w_in=w_in[l], hg_norm=hg_norm[l], da_lambda=da_lambda[l],
                 da_subln=da_subln[l], w_branch_a=w_branch_a[l], w_branch_b=w_branch_b[l], w_out=w_out[l],
                 norm_mix_post=norm_mix_post[l], norm_ffn_pre=norm_ffn_pre[l], w_ffn_gate=w_ffn_gate[l],
                 w_ffn_up=w_ffn_up[l], conv_w=conv_w[l], conv_b=conv_b[l], w_ffn_down=w_ffn_down[l],
                 norm_ffn_post=norm_ffn_post[l])
        yp, kp, vp, sp, cp = _layer(
            yp, pos_p, _diff_attn_prompt,
            jnp.zeros((bp, HG_HEADS, HG_KEY_DIM, HG_VAL_DIM), jnp.float32),
            jnp.zeros((bp, CONV_W - 1, D_FF), x_prompt.dtype), p, lb_all[l], l)
        attend_s = functools.partial(_diff_attn_sample, cache_k=cache_k, cache_v=cache_v,
                                     page_table=page_table, layer=l)
        ys, kss, vss, sss, css = _layer(ys, pos_s, attend_s, state_hgrn[l], state_conv[l], p, lb_all[l], l)
        kp_l.append(kp); vp_l.append(vp); sp_l.append(sp); cp_l.append(cp)
        ks_l.append(kss); vs_l.append(vss); ss_l.append(sss); cs_l.append(css)
    return (yp, ys,
            jnp.stack(kp_l), jnp.stack(vp_l), jnp.stack(sp_l).astype(state_hgrn.dtype), jnp.stack(cp_l),
            jnp.stack(ks_l), jnp.stack(vs_l), jnp.stack(ss_l).astype(state_hgrn.dtype), jnp.stack(cs_l))
```

```python
import functools
import math

import jax
import jax.numpy as jnp
from jax import lax
from jax.experimental import pallas as pl
from jax.experimental.pallas import tpu as pltpu

F32 = jnp.float32
BF16 = jnp.bfloat16

RMS_EPS = 1e-6
ROPE_THETA = 500000.0
HEAD_W = 128
DA_HEAD_DIM = 64
ROT_DIM = DA_HEAD_DIM // 4
PAGE_SIZE = 128
CONV_W = 3
SUBLANES = 8
VMEM_LIMIT = 56 * 1024 * 1024


def _cparams(*sem):
    return pltpu.CompilerParams(dimension_semantics=sem, vmem_limit_bytes=VMEM_LIMIT)


def _sigmoid(x):
    return 1.0 / (1.0 + jnp.exp(-x))


def _rms(x, w):
    return x * lax.rsqrt(jnp.mean(x * x, axis=-1, keepdims=True) + RMS_EPS) * w


def _dot(a, b):
    return jnp.dot(a, b, preferred_element_type=F32)


def _dot_nt(a, b):
    return lax.dot_general(a, b, (((1,), (1,)), ((), ())), preferred_element_type=F32)


def _dot_tn(a, b):
    return lax.dot_general(a, b, (((0,), (0,)), ((), ())), preferred_element_type=F32)


def _norm_kernel(x_ref, w_ref, o_ref):
    o_ref[...] = _rms(x_ref[...], w_ref[...]).astype(o_ref.dtype)


def _norm(x, w, tm):
    t, d = x.shape
    return pl.pallas_call(
        _norm_kernel,
        grid=(t // tm,),
        in_specs=[pl.BlockSpec((tm, d), lambda i: (i, 0)), pl.BlockSpec((1, d), lambda i: (0, 0))],
        out_specs=pl.BlockSpec((tm, d), lambda i: (i, 0)),
        out_shape=jax.ShapeDtypeStruct((t, d), BF16),
        compiler_params=_cparams("parallel"),
    )(x, w.reshape(1, d))


def _rotary_tile(a, cos, sin_lo, sin_hi):
    return a * cos + pltpu.roll(a, ROT_DIM // 2, 1) * sin_hi + pltpu.roll(a, HEAD_W - ROT_DIM // 2, 1) * sin_lo


def _proj_kernel(*refs, mode, n_tab):
    h_ref, w_ref = refs[0], refs[1]
    tabs = refs[2:2 + n_tab]
    outs = refs[2 + n_tab:-1]
    wbf_ref = refs[-1]

    @pl.when(pl.program_id(1) == 0)
    def _():
        wbf_ref[...] = w_ref[...].astype(BF16)

    acc = _dot(h_ref[...], wbf_ref[...])
    if mode == "plain":
        outs[0][...] = acc
    elif mode == "sigmoid":
        outs[0][...] = _sigmoid(acc)
    elif mode == "v":
        outs[0][...] = acc
        outs[1][...] = acc.astype(BF16)
    else:
        cos, sin_lo, sin_hi = (t[...] for t in tabs)
        for g in range(acc.shape[1] // HEAD_W):
            sl = slice(g * HEAD_W, (g + 1) * HEAD_W)
            r = _rotary_tile(acc[:, sl], cos, sin_lo, sin_hi)
            if mode == "rot_q":
                outs[0][:, sl] = (r * (DA_HEAD_DIM ** -0.5)).astype(BF16)
            else:
                outs[0][:, sl] = r
                outs[1][:, sl] = r.astype(BF16)


def _in_proj(h, w_in, col_off, n_cols, mode, tm, tn, tables=()):
    t, k = h.shape
    out_dtypes = {"plain": (F32,), "sigmoid": (F32,), "v": (F32, BF16), "rot_q": (BF16,), "rot_k": (F32, BF16)}[mode]
    blk_off = col_off // tn
    in_specs = [pl.BlockSpec((tm, k), lambda j, i: (i, 0)),
                pl.BlockSpec((k, tn), lambda j, i: (0, j + blk_off))]
    in_specs += [pl.BlockSpec((tm, HEAD_W), lambda j, i: (i, 0)) for _ in tables]
    outs = pl.pallas_call(
        functools.partial(_proj_kernel, mode=mode, n_tab=len(tables)),
        grid=(n_cols // tn, t // tm),
        in_specs=in_specs,
        out_specs=[pl.BlockSpec((tm, tn), lambda j, i: (i, j)) for _ in out_dtypes],
        out_shape=[jax.ShapeDtypeStruct((t, n_cols), dt) for dt in out_dtypes],
        scratch_shapes=[pltpu.VMEM((k, tn), BF16)],
        compiler_params=_cparams("arbitrary", "arbitrary"),
    )(h, w_in, *tables)
    return outs


def _rotary_tables(pos):
    half = ROT_DIM // 2
    inv_freq = jnp.power(jnp.float32(ROPE_THETA), -jnp.arange(half, dtype=F32) * (2.0 / ROT_DIM))
    ang = pos.astype(F32)[:, None] * inv_freq[None, :]
    cos, sin = jnp.cos(ang), jnp.sin(ang)
    lane = jnp.arange(HEAD_W) % DA_HEAD_DIM
    idx = lane % half
    is_lo, is_hi = lane < half, (lane >= half) & (lane < ROT_DIM)
    cos_t = jnp.where((is_lo | is_hi)[None, :], cos[:, idx], 1.0)
    sin_lo = jnp.where(is_lo[None, :], -sin[:, idx], 0.0)
    sin_hi = jnp.where(is_hi[None, :], sin[:, idx], 0.0)
    return cos_t, sin_lo, sin_hi


def _hgrn_kernel(hq_ref, hf_ref, hi_ref, hg_ref, lb_ref, nw_ref, s0_ref, o_ref, sout_ref,
                 st_ref, b_ref, k_ref, *, chunk, n_chunks, l_valid):
    c_len = chunk
    step = pl.program_id(2)

    @pl.when(step == 0)
    def _():
        st_ref[...] = s0_ref[...].T

    lb = lb_ref[...]
    row = lax.broadcasted_iota(jnp.int32, (c_len, c_len), 0)
    col = lax.broadcasted_iota(jnp.int32, (c_len, c_len), 1)
    tri = (col <= row).astype(F32)
    sub_t = lax.broadcasted_iota(jnp.int32, (SUBLANES, HEAD_W), 0)

    def one_chunk(c, carry):
        r0 = pl.multiple_of(c * c_len, c_len)
        rows = pl.ds(r0, c_len)
        hq, hf, hi, hg = hq_ref[rows, :], hf_ref[rows, :], hi_ref[rows, :], hg_ref[rows, :]
        f = lb + (1.0 - lb) * _sigmoid(hf)
        logf = jnp.log(f)
        kk = 1.0 - f
        if l_valid < c_len * n_chunks:
            valid = (lax.broadcasted_iota(jnp.int32, (c_len, HEAD_W), 0) + r0) < l_valid
            logf = jnp.where(valid, logf, 0.0)
            kk = jnp.where(valid, kk, 0.0)
        q = hq * _sigmoid(hq)
        b = jnp.dot(tri, logf, preferred_element_type=F32, precision=lax.Precision.HIGHEST)
        b_ref[...] = b
        k_ref[...] = kk
        b_last = b[c_len - 1:c_len, :]

        diag_blocks = []
        for blk in range(c_len // SUBLANES):
            rs = slice(blk * SUBLANES, (blk + 1) * SUBLANES)
            q_blk, b_blk = q[rs, :], b[rs, :]
            od = jnp.zeros((SUBLANES, HEAD_W), F32)
            for s in range(SUBLANES):
                r = blk * SUBLANES + s
                b_s = b_ref[pl.ds(r, 1), :]
                k_s = k_ref[pl.ds(r, 1), :]
                v_s = hi_ref[pl.ds(r0 + r, 1), :]
                e = jnp.where(sub_t >= s, jnp.exp(jnp.minimum(b_blk - b_s, 0.0)), 0.0)
                a_col = jnp.sum(q_blk * k_s * e, axis=1, keepdims=True)
                od = od + a_col * v_s
            diag_blocks.append(od)
        o = jnp.concatenate(diag_blocks, axis=0) if len(diag_blocks) > 1 else diag_blocks[0]

        a_off = None
        h = SUBLANES
        while h < c_len:
            grp = 2 * h
            pieces = [jnp.broadcast_to(b_ref[pl.ds(gi * grp + h - 1, 1), :], (grp, HEAD_W))
                      for gi in range(c_len // grp)]
            ref = jnp.concatenate(pieces, axis=0) if len(pieces) > 1 else pieces[0]
            q_h = q * jnp.exp(jnp.minimum(b - ref, 0.0))
            k_h = kk * jnp.exp(jnp.minimum(ref - b, 0.0))
            a_h = _dot_nt(q_h.astype(BF16), k_h.astype(BF16))
            same_blk = jnp.bitwise_and(jnp.bitwise_xor(row, col), -grp) == 0
            a_h = jnp.where(same_blk, a_h, 0.0)
            a_h = jnp.where(jnp.bitwise_and(row, h) != 0, a_h, 0.0)
            a_h = jnp.where(jnp.bitwise_and(col, h) == 0, a_h, 0.0)
            a_off = a_h if a_off is None else a_off + a_h
            h = grp
        v_bf = hi.astype(BF16)
        if a_off is not None:
            o = o + _dot(a_off.astype(BF16), v_bf)

        st = st_ref[...]
        o = o + _dot_nt((q * jnp.exp(b)).astype(BF16), st.astype(BF16))
        k_dec = kk * jnp.exp(b_last - b)
        st_ref[...] = st * jnp.exp(b_last) + _dot_tn(v_bf, k_dec.astype(BF16))

        o_ref[rows, :] = (_rms(o, nw_ref[...]) * (hg * _sigmoid(hg))).astype(o_ref.dtype)
        return carry

    lax.fori_loop(0, n_chunks, one_chunk, 0)

    @pl.when(step == pl.num_programs(2) - 1)
    def _():
        sout_ref[...] = st_ref[...].T


def _hgrn(gates, lb, norm_w, s0, l_valid, block_len, chunk):
    bsz, seq, width = gates.shape
    n_heads = width // (4 * HEAD_W)

    def gate_spec(g):
        return pl.BlockSpec((None, block_len, HEAD_W), lambda b, h, s: (b, s, g * n_heads + h))

    head_vec = pl.BlockSpec((1, HEAD_W), lambda b, h, s: (0, h))
    state_spec = pl.BlockSpec((None, None, HEAD_W, HEAD_W), lambda b, h, s: (b, h, 0, 0))
    return pl.pallas_call(
        functools.partial(_hgrn_kernel, chunk=chunk, n_chunks=block_len // chunk, l_valid=l_valid),
        grid=(bsz, n_heads, seq // block_len),
        in_specs=[gate_spec(0), gate_spec(1), gate_spec(2), gate_spec(3), head_vec,
                  pl.BlockSpec((1, HEAD_W), lambda b, h, s: (0, 0)), state_spec],
        out_specs=[pl.BlockSpec((None, block_len, HEAD_W), lambda b, h, s: (b, s, h)), state_spec],
        out_shape=[jax.ShapeDtypeStruct((bsz, seq, n_heads * HEAD_W), BF16),
                   jax.ShapeDtypeStruct(s0.shape, F32)],
        scratch_shapes=[pltpu.VMEM((HEAD_W, HEAD_W), F32), pltpu.VMEM((chunk, HEAD_W), F32),
                        pltpu.VMEM((chunk, HEAD_W), F32)],
        compiler_params=_cparams("parallel", "parallel", "arbitrary"),
    )(gates, gates, gates, gates, lb.reshape(1, -1), norm_w.reshape(1, HEAD_W), s0)


def _flash_kernel(q_ref, k_ref, v_ref, lam_ref, nw_ref, o_ref, qs_ref, m_ref, l_ref, acc_ref, *, tq, post_scale):
    qi = pl.program_id(2)
    q = q_ref[...]
    lane = lax.broadcasted_iota(jnp.int32, q.shape, 1)
    zero = jnp.zeros_like(q)
    qs_ref[0:tq, :] = jnp.where(lane < DA_HEAD_DIM, q, zero)
    qs_ref[tq:2 * tq, :] = jnp.where(lane >= DA_HEAD_DIM, q, zero)
    m_ref[...] = jnp.full(m_ref.shape, -jnp.inf, F32)
    l_ref[...] = jnp.zeros(l_ref.shape, F32)
    acc_ref[...] = jnp.zeros(acc_ref.shape, F32)

    def update(k, v, mask):
        s = _dot_nt(qs_ref[...], k)
        if mask is not None:
            s = jnp.where(mask, s, -jnp.inf)
        m_old = m_ref[...]
        m_new = jnp.maximum(m_old, jnp.max(s, axis=1, keepdims=True))
        alpha = jnp.exp(m_old - m_new)
        p = jnp.exp(s - m_new)
        l_ref[...] = alpha * l_ref[...] + jnp.sum(p, axis=1, keepdims=True)
        acc_ref[...] = alpha * acc_ref[...] + _dot(p.astype(BF16), v)
        m_ref[...] = m_new

    def body(ki, carry):
        rows = pl.ds(pl.multiple_of(ki * tq, tq), tq)
        update(k_ref[rows, :], v_ref[rows, :], None)
        return carry

    lax.fori_loop(0, qi, body, 0)
    rows = pl.ds(pl.multiple_of(qi * tq, tq), tq)
    r = lax.broadcasted_iota(jnp.int32, (2 * tq, tq), 0)
    c = lax.broadcasted_iota(jnp.int32, (2 * tq, tq), 1)
    update(k_ref[rows, :], v_ref[rows, :], c <= jnp.where(r >= tq, r - tq, r))

    out = acc_ref[...] / l_ref[...]
    ob = out[0:tq, :] - lam_ref[...] * out[tq:2 * tq, :]
    o_ref[...] = (_rms(ob, nw_ref[...]) * post_scale).astype(o_ref.dtype)


def _flash_attention(q, k, v, lam_vec, norm_w, post_scale, tq):
    bsz, seq, width = q.shape
    n_heads = width // HEAD_W
    kv_spec = pl.BlockSpec((None, seq, HEAD_W), lambda b, h, i: (b, 0, h))
    vec = pl.BlockSpec((1, HEAD_W), lambda b, h, i: (0, 0))
    return pl.pallas_call(
        functools.partial(_flash_kernel, tq=tq, post_scale=post_scale),
        grid=(bsz, n_heads, seq // tq),
        in_specs=[pl.BlockSpec((None, tq, HEAD_W), lambda b, h, i: (b, i, h)), kv_spec, kv_spec, vec, vec],
        out_specs=pl.BlockSpec((None, tq, HEAD_W), lambda b, h, i: (b, i, h)),
        out_shape=jax.ShapeDtypeStruct((bsz, seq, width), BF16),
        scratch_shapes=[pltpu.VMEM((2 * tq, HEAD_W), BF16), pltpu.VMEM((2 * tq, 1), F32),
                        pltpu.VMEM((2 * tq, 1), F32), pltpu.VMEM((2 * tq, HEAD_W), F32)],
        compiler_params=_cparams("parallel", "parallel", "arbitrary"),
    )(q, k, v, lam_vec, norm_w.reshape(1, HEAD_W))


def _lane_to_col(vec):
    n = vec.shape[1]
    r = lax.broadcasted_iota(jnp.int32, (n, n), 0)
    c = lax.broadcasted_iota(jnp.int32, (n, n), 1)
    return jnp.sum(jnp.where(r == c, jnp.broadcast_to(vec, (n, n)), 0.0), axis=1, keepdims=True)


def _paged_kernel(pt_ref, w_ref, *refs, n_pp, n_heads, l_new, post_scale):
    k_refs, v_refs = refs[:n_pp], refs[n_pp:2 * n_pp]
    knew_ref, vnew_ref, lam_ref, nw_ref, o_ref, m_ref, l_ref, acc_ref = refs[2 * n_pp:]
    g = pl.program_id(1)

    @pl.when(g == 0)
    def _():
        m_ref[...] = jnp.full(m_ref.shape, -jnp.inf, F32)
        l_ref[...] = jnp.zeros(l_ref.shape, F32)
        acc_ref[...] = jnp.zeros(acc_ref.shape, F32)

    w = w_ref[...]

    def softmax_step(s_t):
        m_old = m_ref[...]
        m_new = jnp.maximum(m_old, jnp.max(s_t, axis=0, keepdims=True))
        alpha = jnp.exp(m_old - m_new)
        p = jnp.exp(s_t - m_new)
        l_ref[...] = alpha * l_ref[...] + jnp.sum(p, axis=0, keepdims=True)
        m_ref[...] = m_new
        return _lane_to_col(alpha), p

    s_pages = [_dot(k_refs[i][...].astype(BF16), w) for i in range(n_pp)]
    v_pages = [v_refs[i][...].astype(BF16) for i in range(n_pp)]
    alpha_col, p = softmax_step(jnp.concatenate(s_pages, axis=0))
    acc_ref[...] = alpha_col * acc_ref[...] + _dot(p.T.astype(BF16), jnp.concatenate(v_pages, axis=0))

    @pl.when(g == pl.num_programs(1) - 1)
    def _():
        s_new = _dot(knew_ref[...].astype(BF16), w)
        s_row = lax.broadcasted_iota(jnp.int32, s_new.shape, 0)
        t_col = lax.rem(lax.broadcasted_iota(jnp.int32, s_new.shape, 1), l_new)
        alpha_col, p = softmax_step(jnp.where(s_row <= t_col, s_new, -jnp.inf))
        acc = alpha_col * acc_ref[...]
        for s in range(l_new):
            acc = acc + _lane_to_col(p[s:s + 1, :]) * vnew_ref[s:s + 1, :]
        acc_ref[...] = acc
        inv_l = 1.0 / _lane_to_col(l_ref[...])
        lam = lam_ref[...]
        for h in range(n_heads):
            rs = slice(h * 2 * l_new, (h + 1) * 2 * l_new)
            cs = slice(h * HEAD_W, (h + 1) * HEAD_W)
            maps = acc_ref[rs, cs] * inv_l[rs, :]
            ob = maps - lam * pltpu.roll(maps, l_new, 0)
            o_ref[:, cs] = _rms(ob, nw_ref[...]) * post_scale


def _paged_attention(q, k_new, v_new, cache_k, cache_v, page_table, lam_vec, norm_w, post_scale, n_pp):
    bsz, l_new, width = q.shape
    n_heads = width // HEAD_W
    n_pages = page_table.shape[1]
    assert 2 * l_new == SUBLANES and n_heads * 2 * l_new <= HEAD_W and n_pages % n_pp == 0
    q5 = q.reshape(bsz, l_new, n_heads, 2, DA_HEAD_DIM).transpose(0, 2, 3, 4, 1)
    sel = (jnp.eye(n_heads, dtype=BF16)[:, None, :, None] * jnp.eye(2, dtype=BF16)[None, :, None, :])
    w = q5[:, :, :, :, None, None, :] * sel[None, :, :, None, :, :, None]
    w = w.reshape(bsz, width, n_heads * 2 * l_new)
    w = jnp.pad(w, ((0, 0), (0, 0), (0, HEAD_W - w.shape[-1])))
    pad_rows = ((0, 0), (0, SUBLANES - l_new), (0, 0))
    k_new, v_new = jnp.pad(k_new, pad_rows), jnp.pad(v_new, pad_rows)

    def page_spec(i):
        return pl.BlockSpec((None, PAGE_SIZE, width), lambda b, g, pt: (pt[b, g * n_pp + i], 0, 0))

    new_spec = pl.BlockSpec((None, SUBLANES, width), lambda b, g, pt: (b, 0, 0))
    vec = pl.BlockSpec((1, HEAD_W), lambda b, g, pt: (0, 0))
    out = pl.pallas_call(
        functools.partial(_paged_kernel, n_pp=n_pp, n_heads=n_heads, l_new=l_new, post_scale=post_scale),
        grid_spec=pltpu.PrefetchScalarGridSpec(
            num_scalar_prefetch=1,
            grid=(bsz, n_pages // n_pp),
            in_specs=[pl.BlockSpec((None, width, HEAD_W), lambda b, g, pt: (b, 0, 0))]
            + [page_spec(i) for i in range(n_pp)] * 2 + [new_spec, new_spec, vec, vec],
            out_specs=new_spec,
            scratch_shapes=[pltpu.VMEM((1, HEAD_W), F32), pltpu.VMEM((1, HEAD_W), F32),
                            pltpu.VMEM((HEAD_W, width), F32)],
        ),
        out_shape=jax.ShapeDtypeStruct((bsz, SUBLANES, width), F32),
        compiler_params=_cparams("parallel", "arbitrary"),
    )(page_table, w, *([cache_k] * n_pp), *([cache_v] * n_pp), k_new, v_new, lam_vec, norm_w.reshape(1, HEAD_W))
    return out[:, :l_new, :].astype(BF16)


def _merge_kernel(oa_ref, ob_ref, wa_ref, wb_ref, ga_ref, gb_ref, u_ref, wa_bf, wb_bf):
    @pl.when(pl.program_id(1) == 0)
    def _():
        wa_bf[...] = wa_ref[...].astype(BF16)
        wb_bf[...] = wb_ref[...].astype(BF16)

    u = ga_ref[...] * _dot(oa_ref[...], wa_bf[...]) + gb_ref[...] * _dot(ob_ref[...], wb_bf[...])
    u_ref[...] = u.astype(u_ref.dtype)


def _merge(oa, ob, w_a, w_b, sig_gates, tm, tn):
    t, k = oa.shape
    n = w_a.shape[1]
    nj = n // tn
    row = pl.BlockSpec((tm, k), lambda j, i: (i, 0))
    wsp = pl.BlockSpec((k, tn), lambda j, i: (0, j))
    return pl.pallas_call(
        _merge_kernel,
        grid=(nj, t // tm),
        in_specs=[row, row, wsp, wsp, pl.BlockSpec((tm, tn), lambda j, i: (i, j)),
                  pl.BlockSpec((tm, tn), lambda j, i: (i, j + nj))],
        out_specs=pl.BlockSpec((tm, tn), lambda j, i: (i, j)),
        out_shape=jax.ShapeDtypeStruct((t, n), BF16),
        scratch_shapes=[pltpu.VMEM((k, tn), BF16), pltpu.VMEM((k, tn), BF16)],
        compiler_params=_cparams("arbitrary", "arbitrary"),
    )(oa, ob, w_a, w_b, sig_gates, sig_gates)


def _outproj_kernel(u_ref, w_ref, x_ref, nw_post_ref, nw_pre_ref, x1_ref, h2_ref):
    y = _dot(u_ref[...], w_ref[...])
    x1 = x_ref[...] + _rms(y, nw_post_ref[...])
    x1_ref[...] = x1
    h2_ref[...] = _rms(x1, nw_pre_ref[...]).astype(h2_ref.dtype)


def _outproj(u, w_out_bf, x, nw_post, nw_pre, tm):
    t, d = x.shape
    row = pl.BlockSpec((tm, d), lambda i: (i, 0))
    vec = pl.BlockSpec((1, d), lambda i: (0, 0))
    return pl.pallas_call(
        _outproj_kernel,
        grid=(t // tm,),
        in_specs=[row, pl.BlockSpec((d, d), lambda i: (0, 0)), row, vec, vec],
        out_specs=[row, row],
        out_shape=[jax.ShapeDtypeStruct((t, d), F32), jax.ShapeDtypeStruct((t, d), BF16)],
        compiler_params=_cparams("parallel"),
    )(u, w_out_bf, x, nw_post.reshape(1, d), nw_pre.reshape(1, d))


def _gelu_tanh(x):
    return 0.5 * x * (1.0 + jnp.tanh(math.sqrt(2.0 / math.pi) * (x + 0.044715 * (x * x * x))))


def _ffn_up_kernel(*refs, seq_len, tiles_per_seq):
    if tiles_per_seq:
        h_ref, wg_ref, wu_ref, cw_ref, cb_ref, prev_ref, ff_ref, tail_ref, wg_bf, wu_bf, carry_ref = refs
    else:
        h_ref, wg_ref, wu_ref, cw_ref, cb_ref, halo1_ref, halo2_ref, ff_ref, g_ref, wg_bf, wu_bf = refs
    i = pl.program_id(1)

    @pl.when(i == 0)
    def _():
        wg_bf[...] = wg_ref[...].astype(BF16)
        wu_bf[...] = wu_ref[...].astype(BF16)

    h = h_ref[...]
    g = _dot(h, wg_bf[...])
    up = _dot(h, wu_bf[...])
    tm = g.shape[0]
    row = lax.broadcasted_iota(jnp.int32, g.shape, 0)
    g1 = pltpu.roll(g, 1, 0)
    g2 = pltpu.roll(g, 2, 0)
    if tiles_per_seq:
        @pl.when(i % tiles_per_seq == 0)
        def _():
            carry_ref[...] = prev_ref[...]

        p0, p1 = carry_ref[0:1, :], carry_ref[1:2, :]
        g1 = jnp.where(row == 0, p1, g1)
        g2 = jnp.where(row == 0, p0, jnp.where(row == 1, p1, g2))
        carry_ref[...] = g[tm - 2:tm, :]

        @pl.when(i % tiles_per_seq == tiles_per_seq - 1)
        def _():
            tail_ref[...] = g[tm - 2:tm, :]
    else:
        t_in_seq = lax.rem(row, seq_len)
        g1 = jnp.where(t_in_seq >= 1, g1, halo1_ref[...])
        g2 = jnp.where(t_in_seq >= 2, g2, halo2_ref[...])
        g_ref[...] = g
    conv = cb_ref[...] + g2 * cw_ref[0:1, :] + g1 * cw_ref[1:2, :] + g * cw_ref[2:3, :]
    ff_ref[...] = (_gelu_tanh(conv) * up).astype(ff_ref.dtype)


def _ffn_up(h2, w_gate, w_up, conv_w, conv_b, conv_prev, seq_len, tm, tn):
    t, d = h2.shape
    f = w_gate.shape[1]
    bsz = t // seq_len
    row = pl.BlockSpec((tm, d), lambda j, i: (i, 0))
    wsp = pl.BlockSpec((d, tn), lambda j, i: (0, j))
    cw = pl.BlockSpec((CONV_W, tn), lambda j, i: (0, j))
    cb = pl.BlockSpec((1, tn), lambda j, i: (0, j))
    tile = pl.BlockSpec((tm, tn), lambda j, i: (i, j))
    scratch = [pltpu.VMEM((d, tn), BF16), pltpu.VMEM((d, tn), BF16)]
    if seq_len % tm == 0:
        tps = seq_len // tm
        state = pl.BlockSpec((None, CONV_W - 1, tn), lambda j, i: (i // tps, 0, j))
        ff, tail = pl.pallas_call(
            functools.partial(_ffn_up_kernel, seq_len=seq_len, tiles_per_seq=tps),
            grid=(f // tn, t // tm),
            in_specs=[row, wsp, wsp, cw, cb, state],
            out_specs=[tile, state],
            out_shape=[jax.ShapeDtypeStruct((t, f), BF16), jax.ShapeDtypeStruct((bsz, CONV_W - 1, f), F32)],
            scratch_shapes=scratch + [pltpu.VMEM((CONV_W - 1, tn), F32)],
            compiler_params=_cparams("arbitrary", "arbitrary"),
        )(h2, w_gate, w_up, conv_w, conv_b.reshape(1, f), conv_prev)
        return ff, tail
    assert tm % seq_len == 0 and seq_len >= CONV_W - 1
    zeros = jnp.zeros((bsz, seq_len - 1, f), F32)
    halo1 = jnp.concatenate([conv_prev[:, 1:2], zeros], axis=1).reshape(t, f)
    halo2 = jnp.concatenate([conv_prev, zeros[:, 1:]], axis=1).reshape(t, f)
    ff, g = pl.pallas_call(
        functools.partial(_ffn_up_kernel, seq_len=seq_len, tiles_per_seq=0),
        grid=(f // tn, t // tm),
        in_specs=[row, wsp, wsp, cw, cb, tile, tile],
        out_specs=[tile, tile],
        out_shape=[jax.ShapeDtypeStruct((t, f), BF16), jax.ShapeDtypeStruct((t, f), F32)],
        scratch_shapes=scratch,
        compiler_params=_cparams("arbitrary", "arbitrary"),
    )(h2, w_gate, w_up, conv_w, conv_b.reshape(1, f), halo1, halo2)
    return ff, g.reshape(bsz, seq_len, f)[:, seq_len - (CONV_W - 1):, :]


def _ffn_down_kernel(ff_ref, w_ref, x_ref, nw_ref, y_ref, acc_ref):
    k = pl.program_id(1)

    @pl.when(k == 0)
    def _():
        acc_ref[...] = jnp.zeros(acc_ref.shape, F32)

    acc_ref[...] += _dot(ff_ref[...], w_ref[...])

    @pl.when(k == pl.num_programs(1) - 1)
    def _():
        y_ref[...] = x_ref[...] + _rms(acc_ref[...], nw_ref[...])


def _ffn_down(ff, w_down_bf, x1, nw, tm, tk):
    t, f = ff.shape
    d = x1.shape[1]
    row = pl.BlockSpec((tm, d), lambda i, k: (i, 0))
    return pl.pallas_call(
        _ffn_down_kernel,
        grid=(t // tm, f // tk),
        in_specs=[pl.BlockSpec((tm, tk), lambda i, k: (i, k)), pl.BlockSpec((tk, d), lambda i, k: (k, 0)),
                  row, pl.BlockSpec((1, d), lambda i, k: (0, 0))],
        out_specs=row,
        out_shape=jax.ShapeDtypeStruct((t, d), F32),
        scratch_shapes=[pltpu.VMEM((tm, d), F32)],
        compiler_params=_cparams("parallel", "arbitrary"),
    )(ff, w_down_bf, x1, nw.reshape(1, d))


def _tile(n, pref):
    return pref if n % pref == 0 else n


def _layer(x, pos, attend, hg_s0, conv_prev, p, lb, layer_idx):
    bsz, seq, d = x.shape
    t = bsz * seq
    hg_w = p["w_branch_a"].shape[0]
    da_w = p["w_branch_b"].shape[0]
    x2 = x.reshape(t, d)
    tm = _tile(t, 1024)

    h = _norm(x2, p["norm_mix_pre"], _tile(t, 512))
    w_in = p["w_in"]
    gates = _in_proj(h, w_in, 0, 4 * hg_w, "plain", tm, 1024)[0]
    tabs = tuple(jnp.tile(tb, (bsz, 1)) for tb in _rotary_tables(pos))
    q_bf = _in_proj(h, w_in, 4 * hg_w, da_w, "rot_q", tm, 1024, tabs)[0]
    k_rows, k_bf = _in_proj(h, w_in, 4 * hg_w + da_w, da_w, "rot_k", tm, 1024, tabs)
    v_rows, v_bf = _in_proj(h, w_in, 4 * hg_w + 2 * da_w, da_w, "v", tm, 1024)
    sig_gates = _in_proj(h, w_in, 4 * hg_w + 3 * da_w, 2 * d, "sigmoid", tm, 1024)[0]

    gates3 = gates.reshape(bsz, seq, 4 * hg_w)
    if seq % 64 == 0:
        oa, s_new = _hgrn(gates3, lb, p["hg_norm"], hg_s0, seq, _tile(seq, 512), 64)
    else:
        pad = (-seq) % SUBLANES
        gates3 = jnp.pad(gates3, ((0, 0), (0, pad), (0, 0)))
        oa, s_new = _hgrn(gates3, lb, p["hg_norm"], hg_s0, seq, seq + pad, seq + pad)
        oa = oa[:, :seq, :]
    oa = oa.reshape(t, hg_w)

    lam_init = 0.8 - 0.6 * math.exp(-0.3 * layer_idx)
    lam = p["da_lambda"].astype(F32)
    lam_full = jnp.exp(jnp.sum(lam[0] * lam[1])) - jnp.exp(jnp.sum(lam[2] * lam[3])) + lam_init
    lam_vec = jnp.full((1, HEAD_W), lam_full, F32)
    shp = (bsz, seq, da_w)
    ob = attend(q_bf.reshape(shp), k_bf.reshape(shp), v_bf.reshape(shp), k_rows.reshape(shp), v_rows.reshape(shp),
                lam_vec, p["da_subln"], 1.0 - lam_init).reshape(t, da_w)

    u = _merge(oa, ob, p["w_branch_a"], p["w_branch_b"], sig_gates, tm, 512)
    x1, h2 = _outproj(u, p["w_out"].astype(BF16), x2, p["norm_mix_post"], p["norm_ffn_pre"], _tile(t, 512))
    ff, conv_new = _ffn_up(h2, p["w_ffn_gate"], p["w_ffn_up"], p["conv_w"], p["conv_b"], conv_prev, seq, tm, 512)
    y = _ffn_down(ff, p["w_ffn_down"].astype(BF16), x1, p["norm_ffn_post"], _tile(t, 512), 1408)
    n_da_heads = da_w // HEAD_W
    return (y.reshape(bsz, seq, d), k_rows.reshape(bsz, seq, n_da_heads, HEAD_W),
            v_rows.reshape(bsz, seq, n_da_heads, HEAD_W), s_new, conv_new)


def kernel(x_prompt, x_sample, cache_k, cache_v, page_table, state_hgrn, state_conv, norm_mix_pre, w_in, hg_lb_logits, hg_norm, da_lambda, da_subln, w_branch_a, w_branch_b, w_out, norm_mix_post, norm_ffn_pre, w_ffn_gate, w_ffn_up, conv_w, conv_b, w_ffn_down, norm_ffn_post):
    depth = w_in.shape[0]
    lb_all = jnp.cumsum(jax.nn.softmax(hg_lb_logits.astype(F32), axis=0), axis=0)
    bp, lp = x_prompt.shape[:2]
    ls = x_sample.shape[1]
    past_len = page_table.shape[1] * PAGE_SIZE
    pos_p = jnp.arange(lp)
    pos_s = past_len + jnp.arange(ls)
    n_hg_heads = state_hgrn.shape[2]
    d_ff = state_conv.shape[-1]
    yp, ys = x_prompt, x_sample
    outs = [[] for _ in range(8)]
    for l in range(depth):
        p = dict(norm_mix_pre=norm_mix_pre[l], w_in=w_in[l], hg_norm=hg_norm[l], da_lambda=da_lambda[l],
                 da_subln=da_subln[l], w_branch_a=w_branch_a[l], w_branch_b=w_branch_b[l], w_out=w_out[l],
                 norm_mix_post=norm_mix_post[l], norm_ffn_pre=norm_ffn_pre[l], w_ffn_gate=w_ffn_gate[l],
                 w_ffn_up=w_ffn_up[l], conv_w=conv_w[l], conv_b=conv_b[l], w_ffn_down=w_ffn_down[l],
                 norm_ffn_post=norm_ffn_post[l])

        def attend_prompt(q, k, v, k_rows, v_rows, lam_vec, norm_w, post_scale):
            return _flash_attention(q, k, v, lam_vec, norm_w, post_scale, _tile(q.shape[1], 512))

        def attend_sample(q, k, v, k_rows, v_rows, lam_vec, norm_w, post_scale, l=l):
            width = q.shape[-1]
            ck = cache_k[l].reshape(cache_k.shape[1], PAGE_SIZE, width)
            cv = cache_v[l].reshape(cache_v.shape[1], PAGE_SIZE, width)
            return _paged_attention(q, k_rows, v_rows, ck, cv, page_table, lam_vec, norm_w, post_scale,
                                    _tile(page_table.shape[1], 8))

        yp, kp, vp, sp, cp = _layer(
            yp, pos_p, attend_prompt, jnp.zeros((bp, n_hg_heads, HEAD_W, HEAD_W), F32),
            jnp.zeros((bp, CONV_W - 1, d_ff), F32), p, lb_all[l], l)
        ys, kss, vss, sss, css = _layer(ys, pos_s, attend_sample, state_hgrn[l], state_conv[l], p, lb_all[l], l)
        for lst, val in zip(outs, (kp, vp, sp, cp, kss, vss, sss, css)):
            lst.append(val)
    kp, vp, sp, cp, kss, vss, sss, css = (jnp.stack(o) for o in outs)
    return (yp, ys, kp, vp, sp.astype(state_hgrn.dtype), cp, kss, vss, sss.astype(state_hgrn.dtype), css)
```

```python
import functools
import math

import jax
import jax.numpy as jnp
from jax import lax
from jax.experimental import pallas as pl
from jax.experimental.pallas import tpu as pltpu

F32 = jnp.float32
BF16 = jnp.bfloat16

RMS_EPS = 1e-6
ROPE_THETA = 500000.0
HEAD_W = 128
DA_HEAD_DIM = 64
ROT_DIM = DA_HEAD_DIM // 4
PAGE_SIZE = 128
CONV_W = 3
SUBLANES = 8
VMEM_LIMIT = 56 * 1024 * 1024


def _cparams(*sem):
    return pltpu.CompilerParams(dimension_semantics=sem, vmem_limit_bytes=VMEM_LIMIT)


def _sigmoid(x):
    return 1.0 / (1.0 + jnp.exp(-x))


def _rms(x, w):
    return x * lax.rsqrt(jnp.mean(x * x, axis=-1, keepdims=True) + RMS_EPS) * w


def _dot(a, b):
    return jnp.dot(a, b, preferred_element_type=F32)


def _dot_nt(a, b):
    return lax.dot_general(a, b, (((1,), (1,)), ((), ())), preferred_element_type=F32)


def _dot_tn(a, b):
    return lax.dot_general(a, b, (((0,), (0,)), ((), ())), preferred_element_type=F32)


def _norm_kernel(x_ref, w_ref, o_ref):
    o_ref[...] = _rms(x_ref[...], w_ref[...]).astype(o_ref.dtype)


def _norm(x, w, tm):
    t, d = x.shape
    return pl.pallas_call(
        _norm_kernel,
        grid=(t // tm,),
        in_specs=[pl.BlockSpec((tm, d), lambda i: (i, 0)), pl.BlockSpec((1, d), lambda i: (0, 0))],
        out_specs=pl.BlockSpec((tm, d), lambda i: (i, 0)),
        out_shape=jax.ShapeDtypeStruct((t, d), BF16),
        compiler_params=_cparams("parallel"),
        name="pre_norm",
    )(x, w.reshape(1, d))


def _rotary_tile(a, cos, sin_lo, sin_hi):
    return a * cos + pltpu.roll(a, ROT_DIM // 2, 1) * sin_hi + pltpu.roll(a, HEAD_W - ROT_DIM // 2, 1) * sin_lo


def _proj_kernel(*refs, mode, n_tab):
    h_ref, w_ref = refs[0], refs[1]
    tabs = refs[2:2 + n_tab]
    outs = refs[2 + n_tab:-1]
    wbf_ref = refs[-1]

    @pl.when(pl.program_id(1) == 0)
    def _():
        wbf_ref[...] = w_ref[...].astype(BF16)

    acc = _dot(h_ref[...], wbf_ref[...])
    if mode == "plain":
        outs[0][...] = acc
    elif mode == "sigmoid":
        outs[0][...] = _sigmoid(acc)
    elif mode == "v":
        outs[0][...] = acc
        outs[1][...] = acc.astype(BF16)
    else:
        cos, sin_lo, sin_hi = (t[...] for t in tabs)
        for g in range(acc.shape[1] // HEAD_W):
            sl = slice(g * HEAD_W, (g + 1) * HEAD_W)
            r = _rotary_tile(acc[:, sl], cos, sin_lo, sin_hi)
            if mode == "rot_q":
                outs[0][:, sl] = (r * (DA_HEAD_DIM ** -0.5)).astype(BF16)
            else:
                outs[0][:, sl] = r
                outs[1][:, sl] = r.astype(BF16)


def _in_proj(h, w_in, col_off, n_cols, mode, tm, tn, tables=()):
    t, k = h.shape
    out_dtypes = {"plain": (F32,), "sigmoid": (F32,), "v": (F32, BF16), "rot_q": (BF16,), "rot_k": (F32, BF16)}[mode]
    blk_off = col_off // tn
    in_specs = [pl.BlockSpec((tm, k), lambda j, i: (i, 0)),
                pl.BlockSpec((k, tn), lambda j, i: (0, j + blk_off))]
    in_specs += [pl.BlockSpec((tm, HEAD_W), lambda j, i: (i, 0)) for _ in tables]
    outs = pl.pallas_call(
        functools.partial(_proj_kernel, mode=mode, n_tab=len(tables)),
        grid=(n_cols // tn, t // tm),
        in_specs=in_specs,
        out_specs=[pl.BlockSpec((tm, tn), lambda j, i: (i, j)) for _ in out_dtypes],
        out_shape=[jax.ShapeDtypeStruct((t, n_cols), dt) for dt in out_dtypes],
        scratch_shapes=[pltpu.VMEM((k, tn), BF16)],
        compiler_params=_cparams("arbitrary", "arbitrary"),
        name="in_proj_" + mode,
    )(h, w_in, *tables)
    return outs


def _rotary_tables(pos):
    half = ROT_DIM // 2
    inv_freq = jnp.power(jnp.float32(ROPE_THETA), -jnp.arange(half, dtype=F32) * (2.0 / ROT_DIM))
    ang = pos.astype(F32)[:, None] * inv_freq[None, :]
    cos, sin = jnp.cos(ang), jnp.sin(ang)
    lane = jnp.arange(HEAD_W) % DA_HEAD_DIM
    idx = lane % half
    is_lo, is_hi = lane < half, (lane >= half) & (lane < ROT_DIM)
    cos_t = jnp.where((is_lo | is_hi)[None, :], cos[:, idx], 1.0)
    sin_lo = jnp.where(is_lo[None, :], -sin[:, idx], 0.0)
    sin_hi = jnp.where(is_hi[None, :], sin[:, idx], 0.0)
    return cos_t, sin_lo, sin_hi


def _hgrn_kernel(hq_ref, hf_ref, hi_ref, hg_ref, lb_ref, nw_ref, s0_ref, o_ref, sout_ref,
                 st_ref, b_ref, k_ref, v_ref, *, n_heads, chunk, n_chunks, l_valid):
    c_len = chunk
    step = pl.program_id(2)

    @pl.when(step == 0)
    def _():
        for hh in range(n_heads):
            st_ref[hh] = s0_ref[hh].T

    row = lax.broadcasted_iota(jnp.int32, (c_len, c_len), 0)
    col = lax.broadcasted_iota(jnp.int32, (c_len, c_len), 1)
    tri = (col <= row).astype(F32)
    sub_t = lax.broadcasted_iota(jnp.int32, (SUBLANES, HEAD_W), 0)

    def head_chunk(hh, r0):
        rows = pl.ds(r0, c_len)
        lanes = slice(hh * HEAD_W, (hh + 1) * HEAD_W)
        lb = lb_ref[:, lanes]
        hq, hf, hi, hg = hq_ref[rows, lanes], hf_ref[rows, lanes], hi_ref[rows, lanes], hg_ref[rows, lanes]
        f = lb + (1.0 - lb) * _sigmoid(hf)
        logf = jnp.log(f)
        kk = 1.0 - f
        if l_valid < c_len * n_chunks:
            valid = (lax.broadcasted_iota(jnp.int32, (c_len, HEAD_W), 0) + r0) < l_valid
            logf = jnp.where(valid, logf, 0.0)
            kk = jnp.where(valid, kk, 0.0)
        q = hq * _sigmoid(hq)
        b = jnp.dot(tri, logf, preferred_element_type=F32, precision=lax.Precision.HIGHEST)
        b_ref[hh] = b
        k_ref[hh] = kk
        v_ref[hh] = hi
        b_last = b[c_len - 1:c_len, :]

        diag_blocks = []
        for blk in range(c_len // SUBLANES):
            rs = slice(blk * SUBLANES, (blk + 1) * SUBLANES)
            q_blk, b_blk = q[rs, :], b[rs, :]
            od = jnp.zeros((SUBLANES, HEAD_W), F32)
            for s in range(SUBLANES):
                r = blk * SUBLANES + s
                b_s = b_ref[hh, pl.ds(r, 1), :]
                k_s = k_ref[hh, pl.ds(r, 1), :]
                v_s = v_ref[hh, pl.ds(r, 1), :]
                e = jnp.where(sub_t >= s, jnp.exp(jnp.minimum(b_blk - b_s, 0.0)), 0.0)
                a_col = jnp.sum(q_blk * k_s * e, axis=1, keepdims=True)
                od = od + a_col * v_s
            diag_blocks.append(od)
        o = jnp.concatenate(diag_blocks, axis=0) if len(diag_blocks) > 1 else diag_blocks[0]

        a_off = None
        h = SUBLANES
        while h < c_len:
            grp = 2 * h
            pieces = [jnp.broadcast_to(b_ref[hh, pl.ds(gi * grp + h - 1, 1), :], (grp, HEAD_W))
                      for gi in range(c_len // grp)]
            ref = jnp.concatenate(pieces, axis=0) if len(pieces) > 1 else pieces[0]
            q_h = q * jnp.exp(jnp.minimum(b - ref, 0.0))
            k_h = kk * jnp.exp(jnp.minimum(ref - b, 0.0))
            a_h = _dot_nt(q_h.astype(BF16), k_h.astype(BF16))
            same_blk = jnp.bitwise_and(jnp.bitwise_xor(row, col), -grp) == 0
            a_h = jnp.where(same_blk, a_h, 0.0)
            a_h = jnp.where(jnp.bitwise_and(row, h) != 0, a_h, 0.0)
            a_h = jnp.where(jnp.bitwise_and(col, h) == 0, a_h, 0.0)
            a_off = a_h if a_off is None else a_off + a_h
            h = grp
        v_bf = hi.astype(BF16)
        if a_off is not None:
            o = o + _dot(a_off.astype(BF16), v_bf)

        st = st_ref[hh]
        o = o + _dot_nt((q * jnp.exp(b)).astype(BF16), st.astype(BF16))
        k_dec = kk * jnp.exp(b_last - b)
        st_ref[hh] = st * jnp.exp(b_last) + _dot_tn(v_bf, k_dec.astype(BF16))

        o_ref[rows, lanes] = (_rms(o, nw_ref[...]) * (hg * _sigmoid(hg))).astype(o_ref.dtype)

    def one_chunk(c, carry):
        r0 = pl.multiple_of(c * c_len, c_len)
        for hh in range(n_heads):
            head_chunk(hh, r0)
        return carry

    lax.fori_loop(0, n_chunks, one_chunk, 0)

    @pl.when(step == pl.num_programs(2) - 1)
    def _():
        for hh in range(n_heads):
            sout_ref[hh] = st_ref[hh].T


def _hgrn(gates, lb, norm_w, s0, l_valid, block_len, chunk, heads_per_step):
    bsz, seq, width = gates.shape
    n_heads = width // (4 * HEAD_W)
    hb = heads_per_step
    n_grp = n_heads // hb

    def gate_spec(g):
        return pl.BlockSpec((None, block_len, hb * HEAD_W), lambda b, h, s: (b, s, g * n_grp + h))

    head_vec = pl.BlockSpec((1, hb * HEAD_W), lambda b, h, s: (0, h))
    state_spec = pl.BlockSpec((None, hb, HEAD_W, HEAD_W), lambda b, h, s: (b, h, 0, 0))
    return pl.pallas_call(
        functools.partial(_hgrn_kernel, n_heads=hb, chunk=chunk, n_chunks=block_len // chunk, l_valid=l_valid),
        grid=(bsz, n_grp, seq // block_len),
        in_specs=[gate_spec(0), gate_spec(1), gate_spec(2), gate_spec(3), head_vec,
                  pl.BlockSpec((1, HEAD_W), lambda b, h, s: (0, 0)), state_spec],
        out_specs=[pl.BlockSpec((None, block_len, hb * HEAD_W), lambda b, h, s: (b, s, h)), state_spec],
        out_shape=[jax.ShapeDtypeStruct((bsz, seq, n_heads * HEAD_W), BF16),
                   jax.ShapeDtypeStruct(s0.shape, F32)],
        scratch_shapes=[pltpu.VMEM((hb, HEAD_W, HEAD_W), F32)] + [pltpu.VMEM((hb, chunk, HEAD_W), F32)] * 3,
        compiler_params=_cparams("parallel", "parallel", "arbitrary"),
        name="hgrn",
    )(gates, gates, gates, gates, lb.reshape(1, -1), norm_w.reshape(1, HEAD_W), s0)


def _flash_kernel(q_ref, k_ref, v_ref, lam_ref, nw_ref, o_ref, qs_ref, s_ref, m_ref, acc_ref, *, tq, post_scale):
    qi = pl.program_id(2)
    n_rows = 2 * tq
    q = q_ref[...]
    lane = lax.broadcasted_iota(jnp.int32, q.shape, 1)
    zero = jnp.zeros_like(q)
    qs_ref[0:tq, :] = jnp.where(lane < DA_HEAD_DIM, q, zero)
    qs_ref[tq:n_rows, :] = jnp.where(lane >= DA_HEAD_DIM, q, zero)
    m_ref[...] = jnp.full(m_ref.shape, -jnp.inf, F32)
    acc_ref[...] = jnp.zeros(acc_ref.shape, F32)
    lane_reps = tq // HEAD_W
    ones_blk = jnp.ones((tq, HEAD_W), BF16)

    def update(k, v, masked):
        s = _dot_nt(qs_ref[...], k)
        if masked:
            r = lax.broadcasted_iota(jnp.int32, s.shape, 0)
            c = lax.broadcasted_iota(jnp.int32, s.shape, 1)
            s = jnp.where(c <= jnp.where(r >= tq, r - tq, r), s, -jnp.inf)
        s_ref[...] = s
        part = s_ref[:, 0:HEAD_W]
        for j in range(1, lane_reps):
            part = jnp.maximum(part, s_ref[:, j * HEAD_W:(j + 1) * HEAD_W])
        m_old = m_ref[...]
        m_new = jnp.maximum(m_old, jnp.max(part, axis=1, keepdims=True))
        alpha = jnp.exp(m_old - m_new)
        m_ref[...] = m_new
        p = jnp.exp(s_ref[...] - jnp.concatenate([m_new] * lane_reps, axis=1)).astype(BF16)
        pv = _dot(p, jnp.concatenate([v, ones_blk], axis=1))
        acc_ref[...] = jnp.concatenate([alpha, alpha], axis=1) * acc_ref[...] + pv

    def body(ki, carry):
        rows = pl.ds(pl.multiple_of(ki * tq, tq), tq)
        update(k_ref[rows, :], v_ref[rows, :], False)
        return carry

    lax.fori_loop(0, qi, body, 0)
    rows = pl.ds(pl.multiple_of(qi * tq, tq), tq)
    update(k_ref[rows, :], v_ref[rows, :], True)

    out = acc_ref[:, 0:HEAD_W] / acc_ref[:, HEAD_W:2 * HEAD_W]
    ob = out[0:tq, :] - lam_ref[...] * out[tq:n_rows, :]
    o_ref[...] = (_rms(ob, nw_ref[...]) * post_scale).astype(o_ref.dtype)


def _flash_attention(q, k, v, lam_vec, norm_w, post_scale, tq):
    bsz, seq, width = q.shape
    n_heads = width // HEAD_W
    kv_spec = pl.BlockSpec((None, seq, HEAD_W), lambda b, h, i: (b, 0, h))
    vec = pl.BlockSpec((1, HEAD_W), lambda b, h, i: (0, 0))
    return pl.pallas_call(
        functools.partial(_flash_kernel, tq=tq, post_scale=post_scale),
        grid=(bsz, n_heads, seq // tq),
        in_specs=[pl.BlockSpec((None, tq, HEAD_W), lambda b, h, i: (b, i, h)), kv_spec, kv_spec, vec, vec],
        out_specs=pl.BlockSpec((None, tq, HEAD_W), lambda b, h, i: (b, i, h)),
        out_shape=jax.ShapeDtypeStruct((bsz, seq, width), BF16),
        scratch_shapes=[pltpu.VMEM((2 * tq, HEAD_W), BF16), pltpu.VMEM((2 * tq, tq), F32),
                        pltpu.VMEM((2 * tq, HEAD_W), F32), pltpu.VMEM((2 * tq, 2 * HEAD_W), F32)],
        compiler_params=_cparams("parallel", "parallel", "arbitrary"),
        name="flash_attention",
    )(q, k, v, lam_vec, norm_w.reshape(1, HEAD_W))


def _lane_to_col(vec):
    n = vec.shape[1]
    r = lax.broadcasted_iota(jnp.int32, (n, n), 0)
    c = lax.broadcasted_iota(jnp.int32, (n, n), 1)
    return jnp.sum(jnp.where(r == c, jnp.broadcast_to(vec, (n, n)), 0.0), axis=1, keepdims=True)


def _paged_kernel(pt_ref, w_ref, *refs, n_pp, n_heads, l_new, post_scale):
    k_refs, v_refs = refs[:n_pp], refs[n_pp:2 * n_pp]
    knew_ref, vnew_ref, lam_ref, nw_ref, o_ref, m_ref, l_ref, acc_ref = refs[2 * n_pp:]
    g = pl.program_id(1)

    @pl.when(g == 0)
    def _():
        m_ref[...] = jnp.full(m_ref.shape, -jnp.inf, F32)
        l_ref[...] = jnp.zeros(l_ref.shape, F32)
        acc_ref[...] = jnp.zeros(acc_ref.shape, F32)

    w = w_ref[...]

    def softmax_step(s_t):
        m_old = m_ref[...]
        m_new = jnp.maximum(m_old, jnp.max(s_t, axis=0, keepdims=True))
        alpha = jnp.exp(m_old - m_new)
        p = jnp.exp(s_t - m_new)
        l_ref[...] = alpha * l_ref[...] + jnp.sum(p, axis=0, keepdims=True)
        m_ref[...] = m_new
        return _lane_to_col(alpha), p

    def page_rows(ref):
        heads = [ref[pl.ds(h, PAGE_SIZE, stride=n_heads), :].astype(BF16) for h in range(n_heads)]
        return jnp.concatenate(heads, axis=1)

    s_pages = [_dot(page_rows(k_refs[i]), w) for i in range(n_pp)]
    v_pages = [page_rows(v_refs[i]) for i in range(n_pp)]
    alpha_col, p = softmax_step(jnp.concatenate(s_pages, axis=0))
    acc_ref[...] = alpha_col * acc_ref[...] + _dot(p.T.astype(BF16), jnp.concatenate(v_pages, axis=0))

    @pl.when(g == pl.num_programs(1) - 1)
    def _():
        s_new = _dot(knew_ref[...].astype(BF16), w)
        s_row = lax.broadcasted_iota(jnp.int32, s_new.shape, 0)
        t_col = lax.rem(lax.broadcasted_iota(jnp.int32, s_new.shape, 1), l_new)
        alpha_col, p = softmax_step(jnp.where(s_row <= t_col, s_new, -jnp.inf))
        acc = alpha_col * acc_ref[...]
        for s in range(l_new):
            acc = acc + _lane_to_col(p[s:s + 1, :]) * vnew_ref[s:s + 1, :]
        acc_ref[...] = acc
        inv_l = 1.0 / _lane_to_col(l_ref[...])
        lam = lam_ref[...]
        for h in range(n_heads):
            rs = slice(h * 2 * l_new, (h + 1) * 2 * l_new)
            cs = slice(h * HEAD_W, (h + 1) * HEAD_W)
            maps = acc_ref[rs, cs] * inv_l[rs, :]
            ob = maps - lam * pltpu.roll(maps, l_new, 0)
            o_ref[:, cs] = _rms(ob, nw_ref[...]) * post_scale


def _paged_attention(q, k_new, v_new, cache_k, cache_v, page_table, lam_vec, norm_w, post_scale, n_pp):
    bsz, l_new, width = q.shape
    n_heads = width // HEAD_W
    n_pages = page_table.shape[1]
    assert 2 * l_new == SUBLANES and n_heads * 2 * l_new <= HEAD_W and n_pages % n_pp == 0
    q5 = q.reshape(bsz, l_new, n_heads, 2, DA_HEAD_DIM).transpose(0, 2, 3, 4, 1)
    sel = (jnp.eye(n_heads, dtype=BF16)[:, None, :, None] * jnp.eye(2, dtype=BF16)[None, :, None, :])
    w = q5[:, :, :, :, None, None, :] * sel[None, :, :, None, :, :, None]
    w = w.reshape(bsz, width, n_heads * 2 * l_new)
    w = jnp.pad(w, ((0, 0), (0, 0), (0, HEAD_W - w.shape[-1])))
    pad_rows = ((0, 0), (0, SUBLANES - l_new), (0, 0))
    k_new, v_new = jnp.pad(k_new, pad_rows), jnp.pad(v_new, pad_rows)

    def page_spec(i):
        return pl.BlockSpec((None, PAGE_SIZE * n_heads, HEAD_W), lambda b, g, pt: (pt[b, g * n_pp + i], 0, 0))

    new_spec = pl.BlockSpec((None, SUBLANES, width), lambda b, g, pt: (b, 0, 0))
    vec = pl.BlockSpec((1, HEAD_W), lambda b, g, pt: (0, 0))
    out = pl.pallas_call(
        functools.partial(_paged_kernel, n_pp=n_pp, n_heads=n_heads, l_new=l_new, post_scale=post_scale),
        grid_spec=pltpu.PrefetchScalarGridSpec(
            num_scalar_prefetch=1,
            grid=(bsz, n_pages // n_pp),
            in_specs=[pl.BlockSpec((None, width, HEAD_W), lambda b, g, pt: (b, 0, 0))]
            + [page_spec(i) for i in range(n_pp)] * 2 + [new_spec, new_spec, vec, vec],
            out_specs=new_spec,
            scratch_shapes=[pltpu.VMEM((1, HEAD_W), F32), pltpu.VMEM((1, HEAD_W), F32),
                            pltpu.VMEM((HEAD_W, width), F32)],
        ),
        out_shape=jax.ShapeDtypeStruct((bsz, SUBLANES, width), F32),
        compiler_params=_cparams("parallel", "arbitrary"),
        name="paged_attention",
    )(page_table, w, *([cache_k] * n_pp), *([cache_v] * n_pp), k_new, v_new, lam_vec, norm_w.reshape(1, HEAD_W))
    return out[:, :l_new, :].astype(BF16)


def _merge_kernel(oa_ref, ob_ref, wa_ref, wb_ref, ga_ref, gb_ref, u_ref, wa_bf, wb_bf):
    @pl.when(pl.program_id(1) == 0)
    def _():
        wa_bf[...] = wa_ref[...].astype(BF16)
        wb_bf[...] = wb_ref[...].astype(BF16)

    u = ga_ref[...] * _dot(oa_ref[...], wa_bf[...]) + gb_ref[...] * _dot(ob_ref[...], wb_bf[...])
    u_ref[...] = u.astype(u_ref.dtype)


def _merge(oa, ob, w_a, w_b, sig_gates, tm, tn):
    t, k = oa.shape
    n = w_a.shape[1]
    nj = n // tn
    row = pl.BlockSpec((tm, k), lambda j, i: (i, 0))
    wsp = pl.BlockSpec((k, tn), lambda j, i: (0, j))
    return pl.pallas_call(
        _merge_kernel,
        grid=(nj, t // tm),
        in_specs=[row, row, wsp, wsp, pl.BlockSpec((tm, tn), lambda j, i: (i, j)),
                  pl.BlockSpec((tm, tn), lambda j, i: (i, j + nj))],
        out_specs=pl.BlockSpec((tm, tn), lambda j, i: (i, j)),
        out_shape=jax.ShapeDtypeStruct((t, n), BF16),
        scratch_shapes=[pltpu.VMEM((k, tn), BF16), pltpu.VMEM((k, tn), BF16)],
        compiler_params=_cparams("arbitrary", "arbitrary"),
        name="branch_merge",
    )(oa, ob, w_a, w_b, sig_gates, sig_gates)


def _outproj_kernel(u_ref, w_ref, x_ref, nw_post_ref, nw_pre_ref, x1_ref, h2_ref):
    y = _dot(u_ref[...], w_ref[...])
    x1 = x_ref[...] + _rms(y, nw_post_ref[...])
    x1_ref[...] = x1
    h2_ref[...] = _rms(x1, nw_pre_ref[...]).astype(h2_ref.dtype)


def _outproj(u, w_out_bf, x, nw_post, nw_pre, tm):
    t, d = x.shape
    row = pl.BlockSpec((tm, d), lambda i: (i, 0))
    vec = pl.BlockSpec((1, d), lambda i: (0, 0))
    return pl.pallas_call(
        _outproj_kernel,
        grid=(t // tm,),
        in_specs=[row, pl.BlockSpec((d, d), lambda i: (0, 0)), row, vec, vec],
        out_specs=[row, row],
        out_shape=[jax.ShapeDtypeStruct((t, d), F32), jax.ShapeDtypeStruct((t, d), BF16)],
        compiler_params=_cparams("parallel"),
        name="out_proj",
    )(u, w_out_bf, x, nw_post.reshape(1, d), nw_pre.reshape(1, d))


def _gelu_tanh(x):
    return 0.5 * x * (1.0 + jnp.tanh(math.sqrt(2.0 / math.pi) * (x + 0.044715 * (x * x * x))))


def _ffn_up_kernel(*refs, seq_len, tiles_per_seq):
    if tiles_per_seq:
        h_ref, wg_ref, wu_ref, cw_ref, cb_ref, prev_ref, ff_ref, tail_ref, wg_bf, wu_bf, carry_ref = refs
    else:
        h_ref, wg_ref, wu_ref, cw_ref, cb_ref, halo1_ref, halo2_ref, ff_ref, g_ref, wg_bf, wu_bf = refs
    i = pl.program_id(1)

    @pl.when(i == 0)
    def _():
        wg_bf[...] = wg_ref[...].astype(BF16)
        wu_bf[...] = wu_ref[...].astype(BF16)

    h = h_ref[...]
    g = _dot(h, wg_bf[...])
    up = _dot(h, wu_bf[...])
    tm = g.shape[0]
    row = lax.broadcasted_iota(jnp.int32, g.shape, 0)
    g1 = pltpu.roll(g, 1, 0)
    g2 = pltpu.roll(g, 2, 0)
    if tiles_per_seq:
        @pl.when(i % tiles_per_seq == 0)
        def _():
            carry_ref[...] = prev_ref[...]

        p0, p1 = carry_ref[0:1, :], carry_ref[1:2, :]
        g1 = jnp.where(row == 0, p1, g1)
        g2 = jnp.where(row == 0, p0, jnp.where(row == 1, p1, g2))
        carry_ref[...] = g[tm - 2:tm, :]

        @pl.when(i % tiles_per_seq == tiles_per_seq - 1)
        def _():
            tail_ref[...] = g[tm - 2:tm, :]
    else:
        t_in_seq = lax.rem(row, seq_len)
        g1 = jnp.where(t_in_seq >= 1, g1, halo1_ref[...])
        g2 = jnp.where(t_in_seq >= 2, g2, halo2_ref[...])
        g_ref[...] = g
    conv = cb_ref[...] + g2 * cw_ref[0:1, :] + g1 * cw_ref[1:2, :] + g * cw_ref[2:3, :]
    ff_ref[...] = (_gelu_tanh(conv) * up).astype(ff_ref.dtype)


def _ffn_up(h2, w_gate, w_up, conv_w, conv_b, conv_prev, seq_len, tm, tn):
    t, d = h2.shape
    f = w_gate.shape[1]
    bsz = t // seq_len
    row = pl.BlockSpec((tm, d), lambda j, i: (i, 0))
    wsp = pl.BlockSpec((d, tn), lambda j, i: (0, j))
    cw = pl.BlockSpec((CONV_W, tn), lambda j, i: (0, j))
    cb = pl.BlockSpec((1, tn), lambda j, i: (0, j))
    tile = pl.BlockSpec((tm, tn), lambda j, i: (i, j))
    scratch = [pltpu.VMEM((d, tn), BF16), pltpu.VMEM((d, tn), BF16)]
    if seq_len % tm == 0:
        tps = seq_len // tm
        state = pl.BlockSpec((None, CONV_W - 1, tn), lambda j, i: (i // tps, 0, j))
        ff, tail = pl.pallas_call(
            functools.partial(_ffn_up_kernel, seq_len=seq_len, tiles_per_seq=tps),
            grid=(f // tn, t // tm),
            in_specs=[row, wsp, wsp, cw, cb, state],
            out_specs=[tile, state],
            out_shape=[jax.ShapeDtypeStruct((t, f), BF16), jax.ShapeDtypeStruct((bsz, CONV_W - 1, f), F32)],
            scratch_shapes=scratch + [pltpu.VMEM((CONV_W - 1, tn), F32)],
            compiler_params=_cparams("arbitrary", "arbitrary"),
            name="ffn_up_long",
        )(h2, w_gate, w_up, conv_w, conv_b.reshape(1, f), conv_prev)
        return ff, tail
    assert tm % seq_len == 0 and seq_len >= CONV_W - 1
    zeros = jnp.zeros((bsz, seq_len - 1, f), F32)
    halo1 = jnp.concatenate([conv_prev[:, 1:2], zeros], axis=1).reshape(t, f)
    halo2 = jnp.concatenate([conv_prev, zeros[:, 1:]], axis=1).reshape(t, f)
    ff, g = pl.pallas_call(
        functools.partial(_ffn_up_kernel, seq_len=seq_len, tiles_per_seq=0),
        grid=(f // tn, t // tm),
        in_specs=[row, wsp, wsp, cw, cb, tile, tile],
        out_specs=[tile, tile],
        out_shape=[jax.ShapeDtypeStruct((t, f), BF16), jax.ShapeDtypeStruct((t, f), F32)],
        scratch_shapes=scratch,
        compiler_params=_cparams("arbitrary", "arbitrary"),
        name="ffn_up_short",
    )(h2, w_gate, w_up, conv_w, conv_b.reshape(1, f), halo1, halo2)
    return ff, g.reshape(bsz, seq_len, f)[:, seq_len - (CONV_W - 1):, :]


def _ffn_down_kernel(ff_ref, w_ref, x_ref, nw_ref, y_ref, acc_ref):
    k = pl.program_id(1)

    @pl.when(k == 0)
    def _():
        acc_ref[...] = jnp.zeros(acc_ref.shape, F32)

    acc_ref[...] += _dot(ff_ref[...], w_ref[...])

    @pl.when(k == pl.num_programs(1) - 1)
    def _():
        y_ref[...] = x_ref[...] + _rms(acc_ref[...], nw_ref[...])


def _ffn_down(ff, w_down_bf, x1, nw, tm, tk):
    t, f = ff.shape
    d = x1.shape[1]
    row = pl.BlockSpec((tm, d), lambda i, k: (i, 0))
    return pl.pallas_call(
        _ffn_down_kernel,
        grid=(t // tm, f // tk),
        in_specs=[pl.BlockSpec((tm, tk), lambda i, k: (i, k)), pl.BlockSpec((tk, d), lambda i, k: (k, 0)),
                  row, pl.BlockSpec((1, d), lambda i, k: (0, 0))],
        out_specs=row,
        out_shape=jax.ShapeDtypeStruct((t, d), F32),
        scratch_shapes=[pltpu.VMEM((tm, d), F32)],
        compiler_params=_cparams("parallel", "arbitrary"),
        name="ffn_down",
    )(ff, w_down_bf, x1, nw.reshape(1, d))


def _tile(n, pref):
    return pref if n % pref == 0 else n


def _layer(x, pos, attend, hg_s0, conv_prev, p, lb, layer_idx):
    bsz, seq, d = x.shape
    t = bsz * seq
    hg_w = p["w_branch_a"].shape[0]
    da_w = p["w_branch_b"].shape[0]
    x2 = x.reshape(t, d)
    tm = _tile(t, 1024)

    h = _norm(x2, p["norm_mix_pre"], _tile(t, 512))
    w_in = p["w_in"]
    gates = _in_proj(h, w_in, 0, 4 * hg_w, "plain", tm, 1024)[0]
    tabs = tuple(jnp.tile(tb, (bsz, 1)) for tb in _rotary_tables(pos))
    q_bf = _in_proj(h, w_in, 4 * hg_w, da_w, "rot_q", tm, 1024, tabs)[0]
    k_rows, k_bf = _in_proj(h, w_in, 4 * hg_w + da_w, da_w, "rot_k", tm, 1024, tabs)
    v_rows, v_bf = _in_proj(h, w_in, 4 * hg_w + 2 * da_w, da_w, "v", tm, 1024)
    sig_gates = _in_proj(h, w_in, 4 * hg_w + 3 * da_w, 2 * d, "sigmoid", tm, 1024)[0]

    gates3 = gates.reshape(bsz, seq, 4 * hg_w)
    if seq % 64 == 0:
        oa, s_new = _hgrn(gates3, lb, p["hg_norm"], hg_s0, seq, _tile(seq, 512), 64, 4)
    else:
        pad = (-seq) % SUBLANES
        gates3 = jnp.pad(gates3, ((0, 0), (0, pad), (0, 0)))
        oa, s_new = _hgrn(gates3, lb, p["hg_norm"], hg_s0, seq, seq + pad, seq + pad, hg_w // HEAD_W)
        oa = oa[:, :seq, :]
    oa = oa.reshape(t, hg_w)

    lam_init = 0.8 - 0.6 * math.exp(-0.3 * layer_idx)
    lam = p["da_lambda"].astype(F32)
    lam_full = jnp.exp(jnp.sum(lam[0] * lam[1])) - jnp.exp(jnp.sum(lam[2] * lam[3])) + lam_init
    lam_vec = jnp.full((1, HEAD_W), lam_full, F32)
    shp = (bsz, seq, da_w)
    ob = attend(q_bf.reshape(shp), k_bf.reshape(shp), v_bf.reshape(shp), k_rows.reshape(shp), v_rows.reshape(shp),
                lam_vec, p["da_subln"], 1.0 - lam_init).reshape(t, da_w)

    u = _merge(oa, ob, p["w_branch_a"], p["w_branch_b"], sig_gates, tm, 512)
    x1, h2 = _outproj(u, p["w_out"].astype(BF16), x2, p["norm_mix_post"], p["norm_ffn_pre"], _tile(t, 512))
    ff, conv_new = _ffn_up(h2, p["w_ffn_gate"], p["w_ffn_up"], p["conv_w"], p["conv_b"], conv_prev, seq, tm, 512)
    y = _ffn_down(ff, p["w_ffn_down"].astype(BF16), x1, p["norm_ffn_post"], _tile(t, 512), 2816)
    n_da_heads = da_w // HEAD_W
    return (y.reshape(bsz, seq, d), k_rows.reshape(bsz, seq, n_da_heads, HEAD_W),
            v_rows.reshape(bsz, seq, n_da_heads, HEAD_W), s_new, conv_new)


def kernel(x_prompt, x_sample, cache_k, cache_v, page_table, state_hgrn, state_conv, norm_mix_pre, w_in, hg_lb_logits, hg_norm, da_lambda, da_subln, w_branch_a, w_branch_b, w_out, norm_mix_post, norm_ffn_pre, w_ffn_gate, w_ffn_up, conv_w, conv_b, w_ffn_down, norm_ffn_post):
    depth = w_in.shape[0]
    lb_all = jnp.cumsum(jax.nn.softmax(hg_lb_logits.astype(F32), axis=0), axis=0)
    bp, lp = x_prompt.shape[:2]
    ls = x_sample.shape[1]
    past_len = page_table.shape[1] * PAGE_SIZE
    pos_p = jnp.arange(lp)
    pos_s = past_len + jnp.arange(ls)
    n_hg_heads = state_hgrn.shape[2]
    d_ff = state_conv.shape[-1]
    yp, ys = x_prompt, x_sample
    outs = [[] for _ in range(8)]
    for l in range(depth):
        p = dict(norm_mix_pre=norm_mix_pre[l], w_in=w_in[l], hg_norm=hg_norm[l], da_lambda=da_lambda[l],
                 da_subln=da_subln[l], w_branch_a=w_branch_a[l], w_branch_b=w_branch_b[l], w_out=w_out[l],
                 norm_mix_post=norm_mix_post[l], norm_ffn_pre=norm_ffn_pre[l], w_ffn_gate=w_ffn_gate[l],
                 w_ffn_up=w_ffn_up[l], conv_w=conv_w[l], conv_b=conv_b[l], w_ffn_down=w_ffn_down[l],
                 norm_ffn_post=norm_ffn_post[l])

        def attend_prompt(q, k, v, k_rows, v_rows, lam_vec, norm_w, post_scale):
            return _flash_attention(q, k, v, lam_vec, norm_w, post_scale, _tile(q.shape[1], 512))

        def attend_sample(q, k, v, k_rows, v_rows, lam_vec, norm_w, post_scale, l=l):
            width = q.shape[-1]
            rows_per_page = PAGE_SIZE * (width // HEAD_W)
            ck = cache_k[l].reshape(cache_k.shape[1], rows_per_page, HEAD_W)
            cv = cache_v[l].reshape(cache_v.shape[1], rows_per_page, HEAD_W)
            return _paged_attention(q, k_rows, v_rows, ck, cv, page_table, lam_vec, norm_w, post_scale,
                                    _tile(page_table.shape[1], 8))

        yp, kp, vp, sp, cp = _layer(
            yp, pos_p, attend_prompt, jnp.zeros((bp, n_hg_heads, HEAD_W, HEAD_W), F32),
            jnp.zeros((bp, CONV_W - 1, d_ff), F32), p, lb_all[l], l)
        ys, kss, vss, sss, css = _layer(ys, pos_s, attend_sample, state_hgrn[l], state_conv[l], p, lb_all[l], l)
        for lst, val in zip(outs, (kp, vp, sp, cp, kss, vss, sss, css)):
            lst.append(val)
    kp, vp, sp, cp, kss, vss, sss, css = (jnp.stack(o) for o in outs)
    return (yp, ys, kp, vp, sp.astype(state_hgrn.dtype), cp, kss, vss, sss.astype(state_hgrn.dtype), css)
```

```python
import functools
import math

import jax
import jax.numpy as jnp
from jax import lax
from jax.experimental import pallas as pl
from jax.experimental.pallas import tpu as pltpu

F32 = jnp.float32
BF16 = jnp.bfloat16

RMS_EPS = 1e-6
ROPE_THETA = 500000.0
HEAD_W = 128
DA_HEAD_DIM = 64
ROT_DIM = DA_HEAD_DIM // 4
PAGE_SIZE = 128
CONV_W = 3
SUBLANES = 8
MXU_WIDTH = 256
VMEM_LIMIT = 56 * 1024 * 1024


def _cparams(*sem):
    return pltpu.CompilerParams(dimension_semantics=sem, vmem_limit_bytes=VMEM_LIMIT)


def _sigmoid(x):
    return 1.0 / (1.0 + jnp.exp(-x))


def _rms(x, w):
    return x * lax.rsqrt(jnp.mean(x * x, axis=-1, keepdims=True) + RMS_EPS) * w


def _dot(a, b):
    return jnp.dot(a, b, preferred_element_type=F32)


def _dot_nt(a, b):
    return lax.dot_general(a, b, (((1,), (1,)), ((), ())), preferred_element_type=F32)


def _dot_tn(a, b):
    return lax.dot_general(a, b, (((0,), (0,)), ((), ())), preferred_element_type=F32)


def _norm_kernel(x_ref, w_ref, o_ref):
    o_ref[...] = _rms(x_ref[...], w_ref[...]).astype(o_ref.dtype)


def _norm(x, w, tm):
    t, d = x.shape
    return pl.pallas_call(
        _norm_kernel,
        grid=(t // tm,),
        in_specs=[pl.BlockSpec((tm, d), lambda i: (i, 0)), pl.BlockSpec((1, d), lambda i: (0, 0))],
        out_specs=pl.BlockSpec((tm, d), lambda i: (i, 0)),
        out_shape=jax.ShapeDtypeStruct((t, d), BF16),
        compiler_params=_cparams("parallel"),
        name="pre_norm",
    )(x, w.reshape(1, d))


def _rotary_tile(a, cos, sin_lo, sin_hi):
    return a * cos + pltpu.roll(a, ROT_DIM // 2, 1) * sin_hi + pltpu.roll(a, HEAD_W - ROT_DIM // 2, 1) * sin_lo


def _proj_kernel(*refs, mode, n_tab):
    h_ref, w_ref = refs[0], refs[1]
    tabs = refs[2:2 + n_tab]
    outs = refs[2 + n_tab:-1]
    wbf_ref = refs[-1]

    @pl.when(pl.program_id(1) == 0)
    def _():
        wbf_ref[...] = w_ref[...].astype(BF16)

    acc = _dot(h_ref[...], wbf_ref[...])
    if mode == "plain":
        outs[0][...] = acc
    elif mode == "sigmoid":
        outs[0][...] = _sigmoid(acc)
    elif mode == "v":
        outs[0][...] = acc
        outs[1][...] = acc.astype(BF16)
    else:
        cos, sin_lo, sin_hi = (t[...] for t in tabs)
        for g in range(acc.shape[1] // HEAD_W):
            sl = slice(g * HEAD_W, (g + 1) * HEAD_W)
            r = _rotary_tile(acc[:, sl], cos, sin_lo, sin_hi)
            if mode == "rot_q":
                outs[0][:, sl] = (r * (DA_HEAD_DIM ** -0.5)).astype(BF16)
            else:
                outs[0][:, sl] = r
                outs[1][:, sl] = r.astype(BF16)


def _in_proj(h, w_in, col_off, n_cols, mode, tm, tn, tables=()):
    t, k = h.shape
    out_dtypes = {"plain": (F32,), "sigmoid": (F32,), "v": (F32, BF16), "rot_q": (BF16,), "rot_k": (F32, BF16)}[mode]
    blk_off = col_off // tn
    in_specs = [pl.BlockSpec((tm, k), lambda j, i: (i, 0)),
                pl.BlockSpec((k, tn), lambda j, i: (0, j + blk_off))]
    in_specs += [pl.BlockSpec((tm, HEAD_W), lambda j, i: (i, 0)) for _ in tables]
    outs = pl.pallas_call(
        functools.partial(_proj_kernel, mode=mode, n_tab=len(tables)),
        grid=(n_cols // tn, t // tm),
        in_specs=in_specs,
        out_specs=[pl.BlockSpec((tm, tn), lambda j, i: (i, j)) for _ in out_dtypes],
        out_shape=[jax.ShapeDtypeStruct((t, n_cols), dt) for dt in out_dtypes],
        scratch_shapes=[pltpu.VMEM((k, tn), BF16)],
        compiler_params=_cparams("arbitrary", "arbitrary"),
        name="in_proj_" + mode,
    )(h, w_in, *tables)
    return outs


def _rotary_tables(pos):
    half = ROT_DIM // 2
    inv_freq = jnp.power(jnp.float32(ROPE_THETA), -jnp.arange(half, dtype=F32) * (2.0 / ROT_DIM))
    ang = pos.astype(F32)[:, None] * inv_freq[None, :]
    cos, sin = jnp.cos(ang), jnp.sin(ang)
    lane = jnp.arange(HEAD_W) % DA_HEAD_DIM
    idx = lane % half
    is_lo, is_hi = lane < half, (lane >= half) & (lane < ROT_DIM)
    cos_t = jnp.where((is_lo | is_hi)[None, :], cos[:, idx], 1.0)
    sin_lo = jnp.where(is_lo[None, :], -sin[:, idx], 0.0)
    sin_hi = jnp.where(is_hi[None, :], sin[:, idx], 0.0)
    return cos_t, sin_lo, sin_hi


def _hgrn_kernel(hq_ref, hf_ref, hi_ref, hg_ref, lb_ref, nw_ref, s0_ref, o_ref, sout_ref,
                 st_ref, b_ref, k_ref, v_ref, *, n_heads, chunk, n_chunks, l_valid):
    c_len = chunk
    step = pl.program_id(2)

    @pl.when(step == 0)
    def _():
        for hh in range(n_heads):
            st_ref[hh] = s0_ref[hh].T

    row = lax.broadcasted_iota(jnp.int32, (c_len, c_len), 0)
    col = lax.broadcasted_iota(jnp.int32, (c_len, c_len), 1)
    tri = (col <= row).astype(F32)
    sub_t = lax.broadcasted_iota(jnp.int32, (SUBLANES, HEAD_W), 0)

    def head_chunk(hh, r0):
        rows = pl.ds(r0, c_len)
        lanes = slice(hh * HEAD_W, (hh + 1) * HEAD_W)
        lb = lb_ref[:, lanes]
        hq, hf, hi, hg = hq_ref[rows, lanes], hf_ref[rows, lanes], hi_ref[rows, lanes], hg_ref[rows, lanes]
        f = lb + (1.0 - lb) * _sigmoid(hf)
        logf = jnp.log(f)
        kk = 1.0 - f
        if l_valid < c_len * n_chunks:
            valid = (lax.broadcasted_iota(jnp.int32, (c_len, HEAD_W), 0) + r0) < l_valid
            logf = jnp.where(valid, logf, 0.0)
            kk = jnp.where(valid, kk, 0.0)
        q = hq * _sigmoid(hq)
        b = jnp.dot(tri, logf, preferred_element_type=F32, precision=lax.Precision.HIGHEST)
        b_ref[hh] = b
        k_ref[hh] = kk
        v_ref[hh] = hi
        b_last = b[c_len - 1:c_len, :]

        diag_blocks = []
        for blk in range(c_len // SUBLANES):
            rs = slice(blk * SUBLANES, (blk + 1) * SUBLANES)
            q_blk, b_blk = q[rs, :], b[rs, :]
            od = jnp.zeros((SUBLANES, HEAD_W), F32)
            for s in range(SUBLANES):
                r = blk * SUBLANES + s
                b_s = b_ref[hh, pl.ds(r, 1), :]
                k_s = k_ref[hh, pl.ds(r, 1), :]
                v_s = v_ref[hh, pl.ds(r, 1), :]
                e = jnp.where(sub_t >= s, jnp.exp(jnp.minimum(b_blk - b_s, 0.0)), 0.0)
                a_col = jnp.sum(q_blk * k_s * e, axis=1, keepdims=True)
                od = od + a_col * v_s
            diag_blocks.append(od)
        o = jnp.concatenate(diag_blocks, axis=0) if len(diag_blocks) > 1 else diag_blocks[0]

        a_off = None
        h = SUBLANES
        while h < c_len:
            grp = 2 * h
            pieces = [jnp.broadcast_to(b_ref[hh, pl.ds(gi * grp + h - 1, 1), :], (grp, HEAD_W))
                      for gi in range(c_len // grp)]
            ref = jnp.concatenate(pieces, axis=0) if len(pieces) > 1 else pieces[0]
            q_h = q * jnp.exp(jnp.minimum(b - ref, 0.0))
            k_h = kk * jnp.exp(jnp.minimum(ref - b, 0.0))
            a_h = _dot_nt(q_h.astype(BF16), k_h.astype(BF16))
            same_blk = jnp.bitwise_and(jnp.bitwise_xor(row, col), -grp) == 0
            a_h = jnp.where(same_blk, a_h, 0.0)
            a_h = jnp.where(jnp.bitwise_and(row, h) != 0, a_h, 0.0)
            a_h = jnp.where(jnp.bitwise_and(col, h) == 0, a_h, 0.0)
            a_off = a_h if a_off is None else a_off + a_h
            h = grp
        v_bf = hi.astype(BF16)
        if a_off is not None:
            o = o + _dot(a_off.astype(BF16), v_bf)

        st = st_ref[hh]
        o = o + _dot_nt((q * jnp.exp(b)).astype(BF16), st.astype(BF16))
        k_dec = kk * jnp.exp(b_last - b)
        st_ref[hh] = st * jnp.exp(b_last) + _dot_tn(v_bf, k_dec.astype(BF16))

        o_ref[rows, lanes] = (_rms(o, nw_ref[...]) * (hg * _sigmoid(hg))).astype(o_ref.dtype)

    def one_chunk(c, carry):
        r0 = pl.multiple_of(c * c_len, c_len)
        for hh in range(n_heads):
            head_chunk(hh, r0)
        return carry

    lax.fori_loop(0, n_chunks, one_chunk, 0)

    @pl.when(step == pl.num_programs(2) - 1)
    def _():
        for hh in range(n_heads):
            sout_ref[hh] = st_ref[hh].T


def _hgrn(gates, lb, norm_w, s0, l_valid, block_len, chunk, heads_per_step):
    bsz, seq, width = gates.shape
    n_heads = width // (4 * HEAD_W)
    hb = heads_per_step
    n_grp = n_heads // hb

    def gate_spec(g):
        return pl.BlockSpec((None, block_len, hb * HEAD_W), lambda b, h, s: (b, s, g * n_grp + h))

    head_vec = pl.BlockSpec((1, hb * HEAD_W), lambda b, h, s: (0, h))
    state_spec = pl.BlockSpec((None, hb, HEAD_W, HEAD_W), lambda b, h, s: (b, h, 0, 0))
    return pl.pallas_call(
        functools.partial(_hgrn_kernel, n_heads=hb, chunk=chunk, n_chunks=block_len // chunk, l_valid=l_valid),
        grid=(bsz, n_grp, seq // block_len),
        in_specs=[gate_spec(0), gate_spec(1), gate_spec(2), gate_spec(3), head_vec,
                  pl.BlockSpec((1, HEAD_W), lambda b, h, s: (0, 0)), state_spec],
        out_specs=[pl.BlockSpec((None, block_len, hb * HEAD_W), lambda b, h, s: (b, s, h)), state_spec],
        out_shape=[jax.ShapeDtypeStruct((bsz, seq, n_heads * HEAD_W), BF16),
                   jax.ShapeDtypeStruct(s0.shape, F32)],
        scratch_shapes=[pltpu.VMEM((hb, HEAD_W, HEAD_W), F32)] + [pltpu.VMEM((hb, chunk, HEAD_W), F32)] * 3,
        compiler_params=_cparams("parallel", "parallel", "arbitrary"),
        name="hgrn",
    )(gates, gates, gates, gates, lb.reshape(1, -1), norm_w.reshape(1, HEAD_W), s0)


def _flash_kernel(q_ref, k_ref, v_ref, lam_ref, nw_ref, o_ref, qs_ref, sa_ref, sb_ref, m_ref, acc_ref,
                  *, tq, post_scale):
    qi = pl.program_id(2)
    n_rows = 2 * tq
    q = q_ref[...]
    lane = lax.broadcasted_iota(jnp.int32, q.shape, 1)
    zero = jnp.zeros_like(q)
    qs_ref[0:tq, :] = jnp.where(lane < DA_HEAD_DIM, q, zero)
    qs_ref[tq:n_rows, :] = jnp.where(lane >= DA_HEAD_DIM, q, zero)
    m_ref[...] = jnp.full(m_ref.shape, -jnp.inf, F32)
    acc_ref[...] = jnp.zeros(acc_ref.shape, F32)
    lane_reps = tq // HEAD_W
    ones_blk = jnp.ones((tq, HEAD_W), BF16)

    def tile_rows(t):
        return pl.ds(pl.multiple_of(t * tq, tq), tq)

    def scores(t, s_ref):
        s_ref[...] = _dot_nt(qs_ref[...], k_ref[tile_rows(t), :])

    def consume(t, s_ref, masked):
        if masked:
            r = lax.broadcasted_iota(jnp.int32, s_ref.shape, 0)
            c = lax.broadcasted_iota(jnp.int32, s_ref.shape, 1)
            s_ref[...] = jnp.where(c <= jnp.where(r >= tq, r - tq, r), s_ref[...], -jnp.inf)
        part = s_ref[:, 0:HEAD_W]
        for j in range(1, lane_reps):
            part = jnp.maximum(part, s_ref[:, j * HEAD_W:(j + 1) * HEAD_W])
        m_old = m_ref[...]
        m_new = jnp.maximum(m_old, jnp.max(part, axis=1, keepdims=True))
        alpha = jnp.exp(m_old - m_new)
        m_ref[...] = m_new
        p = jnp.exp(s_ref[...] - jnp.concatenate([m_new] * lane_reps, axis=1)).astype(BF16)
        pv = _dot(p, jnp.concatenate([v_ref[tile_rows(t), :], ones_blk], axis=1))
        acc_ref[...] = jnp.concatenate([alpha, alpha], axis=1) * acc_ref[...] + pv

    scores(0, sa_ref)

    def pair(j, carry):
        t = 2 * j
        scores(t + 1, sb_ref)
        consume(t, sa_ref, False)
        scores(t + 2, sa_ref)
        consume(t + 1, sb_ref, False)
        return carry

    lax.fori_loop(0, lax.shift_right_logical(qi, 1), pair, 0)
    odd = lax.bitwise_and(qi, 1)

    @pl.when(odd == 1)
    def _():
        scores(qi, sb_ref)
        consume(qi - 1, sa_ref, False)
        consume(qi, sb_ref, True)

    @pl.when(odd == 0)
    def _():
        consume(qi, sa_ref, True)

    out = acc_ref[:, 0:HEAD_W] / acc_ref[:, HEAD_W:2 * HEAD_W]
    ob = out[0:tq, :] - lam_ref[...] * out[tq:n_rows, :]
    o_ref[...] = (_rms(ob, nw_ref[...]) * post_scale).astype(o_ref.dtype)


def _flash_attention(q, k, v, lam_vec, norm_w, post_scale, tq):
    bsz, seq, width = q.shape
    n_heads = width // HEAD_W
    kv_spec = pl.BlockSpec((None, seq, HEAD_W), lambda b, h, i: (b, 0, h))
    vec = pl.BlockSpec((1, HEAD_W), lambda b, h, i: (0, 0))
    return pl.pallas_call(
        functools.partial(_flash_kernel, tq=tq, post_scale=post_scale),
        grid=(bsz, n_heads, seq // tq),
        in_specs=[pl.BlockSpec((None, tq, HEAD_W), lambda b, h, i: (b, i, h)), kv_spec, kv_spec, vec, vec],
        out_specs=pl.BlockSpec((None, tq, HEAD_W), lambda b, h, i: (b, i, h)),
        out_shape=jax.ShapeDtypeStruct((bsz, seq, width), BF16),
        scratch_shapes=[pltpu.VMEM((2 * tq, HEAD_W), BF16), pltpu.VMEM((2 * tq, tq), F32),
                        pltpu.VMEM((2 * tq, tq), F32), pltpu.VMEM((2 * tq, HEAD_W), F32),
                        pltpu.VMEM((2 * tq, 2 * HEAD_W), F32)],
        compiler_params=_cparams("parallel", "parallel", "arbitrary"),
        name="flash_attention",
    )(q, k, v, lam_vec, norm_w.reshape(1, HEAD_W))


def _lane_to_col(vec):
    n = vec.shape[1]
    r = lax.broadcasted_iota(jnp.int32, (n, n), 0)
    c = lax.broadcasted_iota(jnp.int32, (n, n), 1)
    return jnp.sum(jnp.where(r == c, jnp.broadcast_to(vec, (n, n)), 0.0), axis=1, keepdims=True)


def _paged_kernel(pt_ref, w_ref, *refs, n_pp, n_heads, l_new, post_scale):
    k_refs, v_refs = refs[:n_pp], refs[n_pp:2 * n_pp]
    knew_ref, vnew_ref, lam_ref, nw_ref, o_ref, m_ref, l_ref, acc_ref = refs[2 * n_pp:]
    g = pl.program_id(1)

    @pl.when(g == 0)
    def _():
        m_ref[...] = jnp.full(m_ref.shape, -jnp.inf, F32)
        l_ref[...] = jnp.zeros(l_ref.shape, F32)
        acc_ref[...] = jnp.zeros(acc_ref.shape, F32)

    w = w_ref[...]

    def softmax_step(s_t):
        m_old = m_ref[...]
        m_new = jnp.maximum(m_old, jnp.max(s_t, axis=0, keepdims=True))
        alpha = jnp.exp(m_old - m_new)
        p = jnp.exp(s_t - m_new)
        l_ref[...] = alpha * l_ref[...] + jnp.sum(p, axis=0, keepdims=True)
        m_ref[...] = m_new
        return _lane_to_col(alpha), p

    def page_rows(ref):
        heads = [ref[pl.ds(h, PAGE_SIZE, stride=n_heads), :].astype(BF16) for h in range(n_heads)]
        return jnp.concatenate(heads, axis=1)

    s_pages = [_dot(page_rows(k_refs[i]), w) for i in range(n_pp)]
    v_pages = [page_rows(v_refs[i]) for i in range(n_pp)]
    alpha_col, p = softmax_step(jnp.concatenate(s_pages, axis=0))
    acc_ref[...] = alpha_col * acc_ref[...] + _dot(p.T.astype(BF16), jnp.concatenate(v_pages, axis=0))

    @pl.when(g == pl.num_programs(1) - 1)
    def _():
        s_new = _dot(knew_ref[...].astype(BF16), w)
        s_row = lax.broadcasted_iota(jnp.int32, s_new.shape, 0)
        t_col = lax.rem(lax.broadcasted_iota(jnp.int32, s_new.shape, 1), l_new)
        alpha_col, p = softmax_step(jnp.where(s_row <= t_col, s_new, -jnp.inf))
        acc = alpha_col * acc_ref[...]
        for s in range(l_new):
            acc = acc + _lane_to_col(p[s:s + 1, :]) * vnew_ref[s:s + 1, :]
        acc_ref[...] = acc
        inv_l = 1.0 / _lane_to_col(l_ref[...])
        lam = lam_ref[...]
        for h in range(n_heads):
            rs = slice(h * 2 * l_new, (h + 1) * 2 * l_new)
            cs = slice(h * HEAD_W, (h + 1) * HEAD_W)
            maps = acc_ref[rs, cs] * inv_l[rs, :]
            ob = maps - lam * pltpu.roll(maps, l_new, 0)
            o_ref[:, cs] = _rms(ob, nw_ref[...]) * post_scale


def _paged_attention(q, k_new, v_new, cache_k, cache_v, page_table, lam_vec, norm_w, post_scale, n_pp):
    bsz, l_new, width = q.shape
    n_heads = width // HEAD_W
    n_pages = page_table.shape[1]
    assert 2 * l_new == SUBLANES and n_heads * 2 * l_new <= HEAD_W and n_pages % n_pp == 0
    q5 = q.reshape(bsz, l_new, n_heads, 2, DA_HEAD_DIM).transpose(0, 2, 3, 4, 1)
    sel = (jnp.eye(n_heads, dtype=BF16)[:, None, :, None] * jnp.eye(2, dtype=BF16)[None, :, None, :])
    w = q5[:, :, :, :, None, None, :] * sel[None, :, :, None, :, :, None]
    w = w.reshape(bsz, width, n_heads * 2 * l_new)
    w = jnp.pad(w, ((0, 0), (0, 0), (0, HEAD_W - w.shape[-1])))
    pad_rows = ((0, 0), (0, SUBLANES - l_new), (0, 0))
    k_new, v_new = jnp.pad(k_new, pad_rows), jnp.pad(v_new, pad_rows)

    def page_spec(i):
        return pl.BlockSpec((None, PAGE_SIZE * n_heads, HEAD_W), lambda b, g, pt: (pt[b, g * n_pp + i], 0, 0))

    new_spec = pl.BlockSpec((None, SUBLANES, width), lambda b, g, pt: (b, 0, 0))
    vec = pl.BlockSpec((1, HEAD_W), lambda b, g, pt: (0, 0))
    out = pl.pallas_call(
        functools.partial(_paged_kernel, n_pp=n_pp, n_heads=n_heads, l_new=l_new, post_scale=post_scale),
        grid_spec=pltpu.PrefetchScalarGridSpec(
            num_scalar_prefetch=1,
            grid=(bsz, n_pages // n_pp),
            in_specs=[pl.BlockSpec((None, width, HEAD_W), lambda b, g, pt: (b, 0, 0))]
            + [page_spec(i) for i in range(n_pp)] * 2 + [new_spec, new_spec, vec, vec],
            out_specs=new_spec,
            scratch_shapes=[pltpu.VMEM((1, HEAD_W), F32), pltpu.VMEM((1, HEAD_W), F32),
                            pltpu.VMEM((HEAD_W, width), F32)],
        ),
        out_shape=jax.ShapeDtypeStruct((bsz, SUBLANES, width), F32),
        compiler_params=_cparams("parallel", "arbitrary"),
        name="paged_attention",
    )(page_table, w, *([cache_k] * n_pp), *([cache_v] * n_pp), k_new, v_new, lam_vec, norm_w.reshape(1, HEAD_W))
    return out[:, :l_new, :].astype(BF16)


def _merge_kernel(oa_ref, ob_ref, wa_ref, wb_ref, ga_ref, gb_ref, u_ref, wa_bf, wb_bf):
    @pl.when(pl.program_id(1) == 0)
    def _():
        wa_bf[...] = wa_ref[...].astype(BF16)
        wb_bf[...] = wb_ref[...].astype(BF16)

    u = ga_ref[...] * _dot(oa_ref[...], wa_bf[...]) + gb_ref[...] * _dot(ob_ref[...], wb_bf[...])
    u_ref[...] = u.astype(u_ref.dtype)


def _merge(oa, ob, w_a, w_b, sig_gates, tm, tn):
    t, k = oa.shape
    n = w_a.shape[1]
    nj = n // tn
    row = pl.BlockSpec((tm, k), lambda j, i: (i, 0))
    wsp = pl.BlockSpec((k, tn), lambda j, i: (0, j))
    return pl.pallas_call(
        _merge_kernel,
        grid=(nj, t // tm),
        in_specs=[row, row, wsp, wsp, pl.BlockSpec((tm, tn), lambda j, i: (i, j)),
                  pl.BlockSpec((tm, tn), lambda j, i: (i, j + nj))],
        out_specs=pl.BlockSpec((tm, tn), lambda j, i: (i, j)),
        out_shape=jax.ShapeDtypeStruct((t, n), BF16),
        scratch_shapes=[pltpu.VMEM((k, tn), BF16), pltpu.VMEM((k, tn), BF16)],
        compiler_params=_cparams("arbitrary", "arbitrary"),
        name="branch_merge",
    )(oa, ob, w_a, w_b, sig_gates, sig_gates)


def _outproj_kernel(u_ref, w_ref, x_ref, nw_post_ref, nw_pre_ref, x1_ref, h2_ref):
    y = _dot(u_ref[...], w_ref[...])
    x1 = x_ref[...] + _rms(y, nw_post_ref[...])
    x1_ref[...] = x1
    h2_ref[...] = _rms(x1, nw_pre_ref[...]).astype(h2_ref.dtype)


def _outproj(u, w_out_bf, x, nw_post, nw_pre, tm):
    t, d = x.shape
    row = pl.BlockSpec((tm, d), lambda i: (i, 0))
    vec = pl.BlockSpec((1, d), lambda i: (0, 0))
    return pl.pallas_call(
        _outproj_kernel,
        grid=(t // tm,),
        in_specs=[row, pl.BlockSpec((d, d), lambda i: (0, 0)), row, vec, vec],
        out_specs=[row, row],
        out_shape=[jax.ShapeDtypeStruct((t, d), F32), jax.ShapeDtypeStruct((t, d), BF16)],
        compiler_params=_cparams("parallel"),
        name="out_proj",
    )(u, w_out_bf, x, nw_post.reshape(1, d), nw_pre.reshape(1, d))


def _gelu_tanh(x):
    return 0.5 * x * (1.0 + jnp.tanh(math.sqrt(2.0 / math.pi) * (x + 0.044715 * (x * x * x))))


def _ffn_up_kernel(*refs, seq_len, tiles_per_seq):
    if tiles_per_seq:
        h_ref, wg_ref, wu_ref, cw_ref, cb_ref, prev_ref, ff_ref, tail_ref, wg_bf, wu_bf, carry_ref = refs
    else:
        h_ref, wg_ref, wu_ref, cw_ref, cb_ref, halo1_ref, halo2_ref, ff_ref, g_ref, wg_bf, wu_bf = refs
    i = pl.program_id(1)

    @pl.when(i == 0)
    def _():
        wg_bf[...] = wg_ref[...].astype(BF16)
        wu_bf[...] = wu_ref[...].astype(BF16)

    if tiles_per_seq:
        @pl.when(i % tiles_per_seq == 0)
        def _():
            carry_ref[...] = prev_ref[...]

    h = h_ref[...]
    tm, tn = ff_ref.shape
    cn = min(tn, MXU_WIDTH)
    row = lax.broadcasted_iota(jnp.int32, (tm, cn), 0)
    for c in range(tn // cn):
        cs = slice(c * cn, (c + 1) * cn)
        g = _dot(h, wg_bf[:, cs])
        up = _dot(h, wu_bf[:, cs])
        g1 = pltpu.roll(g, 1, 0)
        g2 = pltpu.roll(g, 2, 0)
        if tiles_per_seq:
            p0, p1 = carry_ref[0:1, cs], carry_ref[1:2, cs]
            g1 = jnp.where(row == 0, p1, g1)
            g2 = jnp.where(row == 0, p0, jnp.where(row == 1, p1, g2))
            carry_ref[:, cs] = g[tm - 2:tm, :]
            tail_ref[:, cs] = g[tm - 2:tm, :]
        else:
            t_in_seq = lax.rem(row, seq_len)
            g1 = jnp.where(t_in_seq >= 1, g1, halo1_ref[:, cs])
            g2 = jnp.where(t_in_seq >= 2, g2, halo2_ref[:, cs])
            g_ref[:, cs] = g
        conv = cb_ref[:, cs] + g2 * cw_ref[0:1, cs] + g1 * cw_ref[1:2, cs] + g * cw_ref[2:3, cs]
        ff_ref[:, cs] = (_gelu_tanh(conv) * up).astype(ff_ref.dtype)


def _ffn_up(h2, w_gate, w_up, conv_w, conv_b, conv_prev, seq_len, tm, tn):
    t, d = h2.shape
    f = w_gate.shape[1]
    bsz = t // seq_len
    row = pl.BlockSpec((tm, d), lambda j, i: (i, 0))
    wsp = pl.BlockSpec((d, tn), lambda j, i: (0, j))
    cw = pl.BlockSpec((CONV_W, tn), lambda j, i: (0, j))
    cb = pl.BlockSpec((1, tn), lambda j, i: (0, j))
    tile = pl.BlockSpec((tm, tn), lambda j, i: (i, j))
    scratch = [pltpu.VMEM((d, tn), BF16), pltpu.VMEM((d, tn), BF16)]
    if seq_len % tm == 0:
        tps = seq_len // tm
        state = pl.BlockSpec((None, CONV_W - 1, tn), lambda j, i: (i // tps, 0, j))
        ff, tail = pl.pallas_call(
            functools.partial(_ffn_up_kernel, seq_len=seq_len, tiles_per_seq=tps),
            grid=(f // tn, t // tm),
            in_specs=[row, wsp, wsp, cw, cb, state],
            out_specs=[tile, state],
            out_shape=[jax.ShapeDtypeStruct((t, f), BF16), jax.ShapeDtypeStruct((bsz, CONV_W - 1, f), F32)],
            scratch_shapes=scratch + [pltpu.VMEM((CONV_W - 1, tn), F32)],
            compiler_params=_cparams("arbitrary", "arbitrary"),
            name="ffn_up_long",
        )(h2, w_gate, w_up, conv_w, conv_b.reshape(1, f), conv_prev)
        return ff, tail
    assert tm % seq_len == 0 and seq_len >= CONV_W - 1
    zeros = jnp.zeros((bsz, seq_len - 1, f), F32)
    halo1 = jnp.concatenate([conv_prev[:, 1:2], zeros], axis=1).reshape(t, f)
    halo2 = jnp.concatenate([conv_prev, zeros[:, 1:]], axis=1).reshape(t, f)
    ff, g = pl.pallas_call(
        functools.partial(_ffn_up_kernel, seq_len=seq_len, tiles_per_seq=0),
        grid=(f // tn, t // tm),
        in_specs=[row, wsp, wsp, cw, cb, tile, tile],
        out_specs=[tile, tile],
        out_shape=[jax.ShapeDtypeStruct((t, f), BF16), jax.ShapeDtypeStruct((t, f), F32)],
        scratch_shapes=scratch,
        compiler_params=_cparams("arbitrary", "arbitrary"),
        name="ffn_up_short",
    )(h2, w_gate, w_up, conv_w, conv_b.reshape(1, f), halo1, halo2)
    return ff, g.reshape(bsz, seq_len, f)[:, seq_len - (CONV_W - 1):, :]


def _ffn_down_kernel(ff_ref, w_ref, x_ref, nw_ref, y_ref, acc_ref):
    k = pl.program_id(1)

    @pl.when(k == 0)
    def _():
        acc_ref[...] = jnp.zeros(acc_ref.shape, F32)

    acc_ref[...] += _dot(ff_ref[...], w_ref[...])

    @pl.when(k == pl.num_programs(1) - 1)
    def _():
        y_ref[...] = x_ref[...] + _rms(acc_ref[...], nw_ref[...])


def _ffn_down(ff, w_down_bf, x1, nw, tm, tk):
    t, f = ff.shape
    d = x1.shape[1]
    row = pl.BlockSpec((tm, d), lambda i, k: (i, 0))
    return pl.pallas_call(
        _ffn_down_kernel,
        grid=(t // tm, f // tk),
        in_specs=[pl.BlockSpec((tm, tk), lambda i, k: (i, k)), pl.BlockSpec((tk, d), lambda i, k: (k, 0)),
                  row, pl.BlockSpec((1, d), lambda i, k: (0, 0))],
        out_specs=row,
        out_shape=jax.ShapeDtypeStruct((t, d), F32),
        scratch_shapes=[pltpu.VMEM((tm, d), F32)],
        compiler_params=_cparams("parallel", "arbitrary"),
        name="ffn_down",
    )(ff, w_down_bf, x1, nw.reshape(1, d))


def _tile(n, pref):
    return pref if n % pref == 0 else n


def _layer(x, pos, attend, hg_s0, conv_prev, p, lb, layer_idx):
    bsz, seq, d = x.shape
    t = bsz * seq
    hg_w = p["w_branch_a"].shape[0]
    da_w = p["w_branch_b"].shape[0]
    x2 = x.reshape(t, d)
    tm = _tile(t, 1024)

    h = _norm(x2, p["norm_mix_pre"], _tile(t, 512))
    w_in = p["w_in"]
    gates = _in_proj(h, w_in, 0, 4 * hg_w, "plain", tm, 1024)[0]
    tabs = tuple(jnp.tile(tb, (bsz, 1)) for tb in _rotary_tables(pos))
    q_bf = _in_proj(h, w_in, 4 * hg_w, da_w, "rot_q", tm, 1024, tabs)[0]
    k_rows, k_bf = _in_proj(h, w_in, 4 * hg_w + da_w, da_w, "rot_k", tm, 1024, tabs)
    v_rows, v_bf = _in_proj(h, w_in, 4 * hg_w + 2 * da_w, da_w, "v", tm, 1024)
    sig_gates = _in_proj(h, w_in, 4 * hg_w + 3 * da_w, 2 * d, "sigmoid", tm, 1024)[0]

    gates3 = gates.reshape(bsz, seq, 4 * hg_w)
    if seq % 64 == 0:
        oa, s_new = _hgrn(gates3, lb, p["hg_norm"], hg_s0, seq, _tile(seq, 512), 64, 8)
    else:
        pad = (-seq) % SUBLANES
        gates3 = jnp.pad(gates3, ((0, 0), (0, pad), (0, 0)))
        oa, s_new = _hgrn(gates3, lb, p["hg_norm"], hg_s0, seq, seq + pad, seq + pad, hg_w // HEAD_W)
        oa = oa[:, :seq, :]
    oa = oa.reshape(t, hg_w)

    lam_init = 0.8 - 0.6 * math.exp(-0.3 * layer_idx)
    lam = p["da_lambda"].astype(F32)
    lam_full = jnp.exp(jnp.sum(lam[0] * lam[1])) - jnp.exp(jnp.sum(lam[2] * lam[3])) + lam_init
    lam_vec = jnp.full((1, HEAD_W), lam_full, F32)
    shp = (bsz, seq, da_w)
    ob = attend(q_bf.reshape(shp), k_bf.reshape(shp), v_bf.reshape(shp), k_rows.reshape(shp), v_rows.reshape(shp),
                lam_vec, p["da_subln"], 1.0 - lam_init).reshape(t, da_w)

    u = _merge(oa, ob, p["w_branch_a"], p["w_branch_b"], sig_gates, tm, 512)
    x1, h2 = _outproj(u, p["w_out"].astype(BF16), x2, p["norm_mix_post"], p["norm_ffn_pre"], _tile(t, 512))
    ff, conv_new = _ffn_up(h2, p["w_ffn_gate"], p["w_ffn_up"], p["conv_w"], p["conv_b"], conv_prev, seq, tm, 512)
    y = _ffn_down(ff, p["w_ffn_down"].astype(BF16), x1, p["norm_ffn_post"], _tile(t, 512), 2816)
    n_da_heads = da_w // HEAD_W
    return (y.reshape(bsz, seq, d), k_rows.reshape(bsz, seq, n_da_heads, HEAD_W),
            v_rows.reshape(bsz, seq, n_da_heads, HEAD_W), s_new, conv_new)


def kernel(x_prompt, x_sample, cache_k, cache_v, page_table, state_hgrn, state_conv, norm_mix_pre, w_in, hg_lb_logits, hg_norm, da_lambda, da_subln, w_branch_a, w_branch_b, w_out, norm_mix_post, norm_ffn_pre, w_ffn_gate, w_ffn_up, conv_w, conv_b, w_ffn_down, norm_ffn_post):
    depth = w_in.shape[0]
    lb_all = jnp.cumsum(jax.nn.softmax(hg_lb_logits.astype(F32), axis=0), axis=0)
    bp, lp = x_prompt.shape[:2]
    ls = x_sample.shape[1]
    past_len = page_table.shape[1] * PAGE_SIZE
    pos_p = jnp.arange(lp)
    pos_s = past_len + jnp.arange(ls)
    n_hg_heads = state_hgrn.shape[2]
    d_ff = state_conv.shape[-1]
    yp, ys = x_prompt, x_sample
    outs = [[] for _ in range(8)]
    for l in range(depth):
        p = dict(norm_mix_pre=norm_mix_pre[l], w_in=w_in[l], hg_norm=hg_norm[l], da_lambda=da_lambda[l],
                 da_subln=da_subln[l], w_branch_a=w_branch_a[l], w_branch_b=w_branch_b[l], w_out=w_out[l],
                 norm_mix_post=norm_mix_post[l], norm_ffn_pre=norm_ffn_pre[l], w_ffn_gate=w_ffn_gate[l],
                 w_ffn_up=w_ffn_up[l], conv_w=conv_w[l], conv_b=conv_b[l], w_ffn_down=w_ffn_down[l],
                 norm_ffn_post=norm_ffn_post[l])

        def attend_prompt(q, k, v, k_rows, v_rows, lam_vec, norm_w, post_scale):
            return _flash_attention(q, k, v, lam_vec, norm_w, post_scale, _tile(q.shape[1], 512))

        def attend_sample(q, k, v, k_rows, v_rows, lam_vec, norm_w, post_scale, l=l):
            width = q.shape[-1]
            rows_per_page = PAGE_SIZE * (width // HEAD_W)
            ck = cache_k[l].reshape(cache_k.shape[1], rows_per_page, HEAD_W)
            cv = cache_v[l].reshape(cache_v.shape[1], rows_per_page, HEAD_W)
            return _paged_attention(q, k_rows, v_rows, ck, cv, page_table, lam_vec, norm_w, post_scale,
                                    _tile(page_table.shape[1], 8))

        yp, kp, vp, sp, cp = _layer(
            yp, pos_p, attend_prompt, jnp.zeros((bp, n_hg_heads, HEAD_W, HEAD_W), F32),
            jnp.zeros((bp, CONV_W - 1, d_ff), F32), p, lb_all[l], l)
        ys, kss, vss, sss, css = _layer(ys, pos_s, attend_sample, state_hgrn[l], state_conv[l], p, lb_all[l], l)
        for lst, val in zip(outs, (kp, vp, sp, cp, kss, vss, sss, css)):
            lst.append(val)
    kp, vp, sp, cp, kss, vss, sss, css = (jnp.stack(o) for o in outs)
    return (yp, ys, kp, vp, sp.astype(state_hgrn.dtype), cp, kss, vss, sss.astype(state_hgrn.dtype), css)
```

```python
import functools
import math

import jax
import jax.numpy as jnp
from jax import lax
from jax.experimental import pallas as pl
from jax.experimental.pallas import tpu as pltpu

F32 = jnp.float32
BF16 = jnp.bfloat16

RMS_EPS = 1e-6
ROPE_THETA = 500000.0
HEAD_W = 128
DA_HEAD_DIM = 64
ROT_DIM = DA_HEAD_DIM // 4
PAGE_SIZE = 128
CONV_W = 3
SUBLANES = 8
MXU_WIDTH = 256
VMEM_LIMIT = 56 * 1024 * 1024


def _cparams(*sem):
    return pltpu.CompilerParams(dimension_semantics=sem, vmem_limit_bytes=VMEM_LIMIT)


def _sigmoid(x):
    return 1.0 / (1.0 + jnp.exp(-x))


def _rms(x, w):
    return x * lax.rsqrt(jnp.mean(x * x, axis=-1, keepdims=True) + RMS_EPS) * w


def _dot(a, b):
    return jnp.dot(a, b, preferred_element_type=F32)


def _dot_nt(a, b):
    return lax.dot_general(a, b, (((1,), (1,)), ((), ())), preferred_element_type=F32)


def _dot_tn(a, b):
    return lax.dot_general(a, b, (((0,), (0,)), ((), ())), preferred_element_type=F32)


def _norm_kernel(x_ref, w_ref, o_ref):
    o_ref[...] = _rms(x_ref[...], w_ref[...]).astype(o_ref.dtype)


def _norm(x, w, tm):
    t, d = x.shape
    return pl.pallas_call(
        _norm_kernel,
        grid=(t // tm,),
        in_specs=[pl.BlockSpec((tm, d), lambda i: (i, 0)), pl.BlockSpec((1, d), lambda i: (0, 0))],
        out_specs=pl.BlockSpec((tm, d), lambda i: (i, 0)),
        out_shape=jax.ShapeDtypeStruct((t, d), BF16),
        compiler_params=_cparams("parallel"),
        name="pre_norm",
    )(x, w.reshape(1, d))


def _rotary_tile(a, cos, sin_lo, sin_hi):
    return a * cos + pltpu.roll(a, ROT_DIM // 2, 1) * sin_hi + pltpu.roll(a, HEAD_W - ROT_DIM // 2, 1) * sin_lo


def _proj_kernel(*refs, mode, n_tab):
    h_ref, w_ref = refs[0], refs[1]
    tabs = refs[2:2 + n_tab]
    outs = refs[2 + n_tab:-1]
    wbf_ref = refs[-1]

    @pl.when(pl.program_id(1) == 0)
    def _():
        wbf_ref[...] = w_ref[...].astype(BF16)

    acc = _dot(h_ref[...], wbf_ref[...])
    n_grp = acc.shape[1] // HEAD_W
    if mode == "plain":
        outs[0][...] = acc
    elif mode == "sigmoid":
        outs[0][...] = _sigmoid(acc).astype(BF16)
    elif mode == "v":
        for g in range(acc.shape[1] // HEAD_W):
            outs[0][pl.ds(g, acc.shape[0], stride=n_grp), :] = acc[:, g * HEAD_W:(g + 1) * HEAD_W]
        outs[1][...] = acc.astype(BF16)
    else:
        cos, sin_lo, sin_hi = (t[...] for t in tabs)
        for g in range(acc.shape[1] // HEAD_W):
            sl = slice(g * HEAD_W, (g + 1) * HEAD_W)
            r = _rotary_tile(acc[:, sl], cos, sin_lo, sin_hi)
            if mode == "rot_q":
                outs[0][:, sl] = (r * (DA_HEAD_DIM ** -0.5)).astype(BF16)
            else:
                outs[0][pl.ds(g, acc.shape[0], stride=n_grp), :] = r
                outs[1][:, sl] = r.astype(BF16)


def _in_proj(h, w_in, col_off, n_cols, mode, tm, tn, tables=()):
    t, k = h.shape
    out_dtypes = {"plain": (F32,), "sigmoid": (BF16,), "v": (F32, BF16), "rot_q": (BF16,), "rot_k": (F32, BF16)}[mode]
    out_specs = [pl.BlockSpec((tm, tn), lambda j, i: (i, j)) for _ in out_dtypes]
    out_shape = [jax.ShapeDtypeStruct((t, n_cols), dt) for dt in out_dtypes]
    if mode in ("v", "rot_k"):
        assert n_cols == tn
        n_grp = tn // HEAD_W
        out_specs[0] = pl.BlockSpec((tm * n_grp, HEAD_W), lambda j, i: (i, 0))
        out_shape[0] = jax.ShapeDtypeStruct((t * n_grp, HEAD_W), F32)
    blk_off = col_off // tn
    in_specs = [pl.BlockSpec((tm, k), lambda j, i: (i, 0)),
                pl.BlockSpec((k, tn), lambda j, i: (0, j + blk_off))]
    tab_tiles = tables[0].shape[0] // tm if tables else 1
    in_specs += [pl.BlockSpec((tm, HEAD_W), lambda j, i: (i % tab_tiles, 0)) for _ in tables]
    outs = pl.pallas_call(
        functools.partial(_proj_kernel, mode=mode, n_tab=len(tables)),
        grid=(n_cols // tn, t // tm),
        in_specs=in_specs,
        out_specs=out_specs,
        out_shape=out_shape,
        scratch_shapes=[pltpu.VMEM((k, tn), BF16)],
        compiler_params=_cparams("arbitrary", "arbitrary"),
        name="in_proj_" + mode,
    )(h, w_in, *tables)
    return outs


def _rotary_tables(pos):
    half = ROT_DIM // 2
    inv_freq = jnp.power(jnp.float32(ROPE_THETA), -jnp.arange(half, dtype=F32) * (2.0 / ROT_DIM))
    ang = pos.astype(F32)[:, None] * inv_freq[None, :]
    cos, sin = jnp.cos(ang), jnp.sin(ang)
    lane = jnp.arange(HEAD_W) % DA_HEAD_DIM
    idx = lane % half
    is_lo, is_hi = lane < half, (lane >= half) & (lane < ROT_DIM)
    cos_t = jnp.where((is_lo | is_hi)[None, :], cos[:, idx], 1.0)
    sin_lo = jnp.where(is_lo[None, :], -sin[:, idx], 0.0)
    sin_hi = jnp.where(is_hi[None, :], sin[:, idx], 0.0)
    return cos_t, sin_lo, sin_hi


def _hgrn_kernel(hq_ref, hf_ref, hi_ref, hg_ref, lb_ref, nw_ref, s0_ref, o_ref, sout_ref,
                 st_ref, b_ref, k_ref, v_ref, *, n_heads, chunk, n_chunks, l_valid):
    c_len = chunk
    step = pl.program_id(2)

    @pl.when(step == 0)
    def _():
        for hh in range(n_heads):
            st_ref[hh] = s0_ref[hh].T

    row = lax.broadcasted_iota(jnp.int32, (c_len, c_len), 0)
    col = lax.broadcasted_iota(jnp.int32, (c_len, c_len), 1)
    tri = jnp.where(col <= row, 1.0, 0.0).astype(BF16)
    level_masks = {}
    h = SUBLANES
    while h < c_len:
        same_blk = jnp.bitwise_and(jnp.bitwise_xor(row, col), -2 * h) == 0
        level_masks[h] = same_blk & (jnp.bitwise_and(row, h) != 0) & (jnp.bitwise_and(col, h) == 0)
        h *= 2
    sub_t = lax.broadcasted_iota(jnp.int32, (SUBLANES, HEAD_W), 0)

    def head_chunk(hh, r0):
        rows = pl.ds(r0, c_len)
        lanes = slice(hh * HEAD_W, (hh + 1) * HEAD_W)
        lb = lb_ref[:, lanes]
        hq, hf, hi, hg = hq_ref[rows, lanes], hf_ref[rows, lanes], hi_ref[rows, lanes], hg_ref[rows, lanes]
        f = lb + (1.0 - lb) * _sigmoid(hf)
        logf = jnp.log(f)
        kk = 1.0 - f
        if l_valid < c_len * n_chunks:
            valid = (lax.broadcasted_iota(jnp.int32, (c_len, HEAD_W), 0) + r0) < l_valid
            logf = jnp.where(valid, logf, 0.0)
            kk = jnp.where(valid, kk, 0.0)
        q = hq * _sigmoid(hq)
        p_hi = logf.astype(BF16)
        rem = logf - p_hi.astype(F32)
        p_mid = rem.astype(BF16)
        p_lo = (rem - p_mid.astype(F32)).astype(BF16)
        csum = _dot(tri, jnp.concatenate([p_hi, p_mid, p_lo], axis=1))
        b = csum[:, 0:HEAD_W] + csum[:, HEAD_W:2 * HEAD_W] + csum[:, 2 * HEAD_W:3 * HEAD_W]
        b_ref[hh] = b
        k_ref[hh] = kk
        v_ref[hh] = hi
        b_last = b[c_len - 1:c_len, :]

        diag_blocks = []
        for blk in range(c_len // SUBLANES):
            rs = slice(blk * SUBLANES, (blk + 1) * SUBLANES)
            q_blk, b_blk = q[rs, :], b[rs, :]
            od = jnp.zeros((SUBLANES, HEAD_W), F32)
            for s in range(SUBLANES):
                r = blk * SUBLANES + s
                b_s = b_ref[hh, pl.ds(r, 1), :]
                k_s = k_ref[hh, pl.ds(r, 1), :]
                v_s = v_ref[hh, pl.ds(r, 1), :]
                e = jnp.where(sub_t >= s, jnp.exp(b_blk - b_s), 0.0)
                a_col = jnp.sum(q_blk * k_s * e, axis=1, keepdims=True)
                od = od + a_col * v_s
            diag_blocks.append(od)
        o = jnp.concatenate(diag_blocks, axis=0) if len(diag_blocks) > 1 else diag_blocks[0]

        a_off = None
        h = SUBLANES
        while h < c_len:
            grp = 2 * h
            pieces = [jnp.broadcast_to(b_ref[hh, pl.ds(gi * grp + h - 1, 1), :], (grp, HEAD_W))
                      for gi in range(c_len // grp)]
            ref = jnp.concatenate(pieces, axis=0) if len(pieces) > 1 else pieces[0]
            q_h = q * jnp.exp(jnp.minimum(b - ref, 0.0))
            k_h = kk * jnp.exp(jnp.minimum(ref - b, 0.0))
            a_h = _dot_nt(q_h.astype(BF16), k_h.astype(BF16))
            a_h = jnp.where(level_masks[h], a_h, 0.0)
            a_off = a_h if a_off is None else a_off + a_h
            h = grp
        v_bf = hi.astype(BF16)
        if a_off is not None:
            o = o + _dot(a_off.astype(BF16), v_bf)

        st = st_ref[hh]
        o = o + _dot_nt((q * jnp.exp(b)).astype(BF16), st.astype(BF16))
        k_dec = kk * jnp.exp(b_last - b)
        st_ref[hh] = st * jnp.exp(b_last) + _dot_tn(v_bf, k_dec.astype(BF16))

        o_ref[rows, lanes] = (_rms(o, nw_ref[...]) * (hg * _sigmoid(hg))).astype(o_ref.dtype)

    def one_chunk(c, carry):
        r0 = pl.multiple_of(c * c_len, c_len)
        for hh in range(n_heads):
            head_chunk(hh, r0)
        return carry

    lax.fori_loop(0, n_chunks, one_chunk, 0)

    @pl.when(step == pl.num_programs(2) - 1)
    def _():
        for hh in range(n_heads):
            sout_ref[hh] = st_ref[hh].T


def _hgrn(gates, lb, norm_w, s0, l_valid, block_len, chunk, heads_per_step):
    bsz, seq, width = gates.shape
    n_heads = width // (4 * HEAD_W)
    hb = heads_per_step
    n_grp = n_heads // hb

    def gate_spec(g):
        return pl.BlockSpec((None, block_len, hb * HEAD_W), lambda b, h, s: (b, s, g * n_grp + h))

    head_vec = pl.BlockSpec((1, hb * HEAD_W), lambda b, h, s: (0, h))
    state_spec = pl.BlockSpec((None, hb, HEAD_W, HEAD_W), lambda b, h, s: (b, h, 0, 0))
    return pl.pallas_call(
        functools.partial(_hgrn_kernel, n_heads=hb, chunk=chunk, n_chunks=block_len // chunk, l_valid=l_valid),
        grid=(bsz, n_grp, seq // block_len),
        in_specs=[gate_spec(0), gate_spec(1), gate_spec(2), gate_spec(3), head_vec,
                  pl.BlockSpec((1, HEAD_W), lambda b, h, s: (0, 0)), state_spec],
        out_specs=[pl.BlockSpec((None, block_len, hb * HEAD_W), lambda b, h, s: (b, s, h)), state_spec],
        out_shape=[jax.ShapeDtypeStruct((bsz, seq, n_heads * HEAD_W), BF16),
                   jax.ShapeDtypeStruct(s0.shape, F32)],
        scratch_shapes=[pltpu.VMEM((hb, HEAD_W, HEAD_W), F32)] + [pltpu.VMEM((hb, chunk, HEAD_W), F32)] * 3,
        compiler_params=_cparams("parallel", "parallel", "arbitrary"),
        name="hgrn",
    )(gates, gates, gates, gates, lb.reshape(1, -1), norm_w.reshape(1, HEAD_W), s0)


def _flash_kernel(q_ref, k_ref, v_ref, lam_ref, nw_ref, o_ref, qs_ref, sa_ref, sb_ref, m_ref, acc_ref,
                  *, tq, n_heads, post_scale):
    qi = pl.program_id(2)
    n_rows = 2 * tq
    lane = lax.broadcasted_iota(jnp.int32, (tq, HEAD_W), 1)
    zero = jnp.zeros((tq, HEAD_W), BF16)
    for hh in range(n_heads):
        q = q_ref[:, hh * HEAD_W:(hh + 1) * HEAD_W]
        qs_ref[hh, 0:tq, :] = jnp.where(lane < DA_HEAD_DIM, q, zero)
        qs_ref[hh, tq:n_rows, :] = jnp.where(lane >= DA_HEAD_DIM, q, zero)
    m_ref[...] = jnp.full(m_ref.shape, -jnp.inf, F32)
    acc_ref[...] = jnp.zeros(acc_ref.shape, F32)
    lane_reps = tq // HEAD_W
    ones_blk = jnp.ones((tq, HEAD_W), BF16)

    def tile_rows(t):
        return pl.ds(pl.multiple_of(t * tq, tq), tq)

    def scores(t, s_ref):
        for hh in range(n_heads):
            s_ref[hh] = _dot_nt(qs_ref[hh], k_ref[tile_rows(t), hh * HEAD_W:(hh + 1) * HEAD_W])

    def consume(t, s_ref, masked):
        for hh in range(n_heads):
            if masked:
                r = lax.broadcasted_iota(jnp.int32, (n_rows, tq), 0)
                c = lax.broadcasted_iota(jnp.int32, (n_rows, tq), 1)
                s_ref[hh] = jnp.where(c <= jnp.where(r >= tq, r - tq, r), s_ref[hh], -jnp.inf)
            part = s_ref[hh, :, 0:HEAD_W]
            for j in range(1, lane_reps):
                part = jnp.maximum(part, s_ref[hh, :, j * HEAD_W:(j + 1) * HEAD_W])
            m_old = m_ref[hh]
            m_new = jnp.maximum(m_old, jnp.max(part, axis=1, keepdims=True))
            alpha = jnp.exp(m_old - m_new)
            m_ref[hh] = m_new
            p = jnp.exp(s_ref[hh] - jnp.concatenate([m_new] * lane_reps, axis=1)).astype(BF16)
            v = v_ref[tile_rows(t), hh * HEAD_W:(hh + 1) * HEAD_W]
            pv = _dot(p, jnp.concatenate([v, ones_blk], axis=1))
            acc_ref[hh] = jnp.concatenate([alpha, alpha], axis=1) * acc_ref[hh] + pv

    scores(0, sa_ref)

    def pair(j, carry):
        t = 2 * j
        scores(t + 1, sb_ref)
        consume(t, sa_ref, False)
        scores(t + 2, sa_ref)
        consume(t + 1, sb_ref, False)
        return carry

    lax.fori_loop(0, lax.shift_right_logical(qi, 1), pair, 0)
    odd = lax.bitwise_and(qi, 1)

    @pl.when(odd == 1)
    def _():
        scores(qi, sb_ref)
        consume(qi - 1, sa_ref, False)
        consume(qi, sb_ref, True)

    @pl.when(odd == 0)
    def _():
        consume(qi, sa_ref, True)

    for hh in range(n_heads):
        out = acc_ref[hh, :, 0:HEAD_W] / acc_ref[hh, :, HEAD_W:2 * HEAD_W]
        ob = out[0:tq, :] - lam_ref[...] * out[tq:n_rows, :]
        o_ref[:, hh * HEAD_W:(hh + 1) * HEAD_W] = (_rms(ob, nw_ref[...]) * post_scale).astype(o_ref.dtype)


def _flash_attention(q, k, v, lam_vec, norm_w, post_scale, tq, heads_per_step):
    bsz, seq, width = q.shape
    hb = heads_per_step
    kv_spec = pl.BlockSpec((None, seq, hb * HEAD_W), lambda b, h, i: (b, 0, h))
    q_spec = pl.BlockSpec((None, tq, hb * HEAD_W), lambda b, h, i: (b, i, h))
    vec = pl.BlockSpec((1, HEAD_W), lambda b, h, i: (0, 0))
    return pl.pallas_call(
        functools.partial(_flash_kernel, tq=tq, n_heads=hb, post_scale=post_scale),
        grid=(bsz, width // (hb * HEAD_W), seq // tq),
        in_specs=[q_spec, kv_spec, kv_spec, vec, vec],
        out_specs=q_spec,
        out_shape=jax.ShapeDtypeStruct((bsz, seq, width), BF16),
        scratch_shapes=[pltpu.VMEM((hb, 2 * tq, HEAD_W), BF16), pltpu.VMEM((hb, 2 * tq, tq), F32),
                        pltpu.VMEM((hb, 2 * tq, tq), F32), pltpu.VMEM((hb, 2 * tq, HEAD_W), F32),
                        pltpu.VMEM((hb, 2 * tq, 2 * HEAD_W), F32)],
        compiler_params=_cparams("parallel", "parallel", "arbitrary"),
        name="flash_attention",
    )(q, k, v, lam_vec, norm_w.reshape(1, HEAD_W))


def _lane_to_col(vec):
    n = vec.shape[1]
    r = lax.broadcasted_iota(jnp.int32, (n, n), 0)
    c = lax.broadcasted_iota(jnp.int32, (n, n), 1)
    return jnp.sum(jnp.where(r == c, jnp.broadcast_to(vec, (n, n)), 0.0), axis=1, keepdims=True)


def _paged_kernel(pt_ref, w_ref, *refs, n_pp, n_heads, l_new, post_scale):
    k_refs, v_refs = refs[:n_pp], refs[n_pp:2 * n_pp]
    knew_ref, vnew_ref, lam_ref, nw_ref, o_ref, m_ref, l_ref, acc_ref = refs[2 * n_pp:]
    g = pl.program_id(1)

    @pl.when(g == 0)
    def _():
        m_ref[...] = jnp.full(m_ref.shape, -jnp.inf, F32)
        l_ref[...] = jnp.zeros(l_ref.shape, F32)
        acc_ref[...] = jnp.zeros(acc_ref.shape, F32)

    w = w_ref[...]

    def softmax_step(s_t):
        m_old = m_ref[...]
        m_new = jnp.maximum(m_old, jnp.max(s_t, axis=0, keepdims=True))
        alpha = jnp.exp(m_old - m_new)
        p = jnp.exp(s_t - m_new)
        l_ref[...] = alpha * l_ref[...] + jnp.sum(p, axis=0, keepdims=True)
        m_ref[...] = m_new
        return _lane_to_col(alpha), p

    def page_rows(ref):
        heads = [ref[pl.ds(h, PAGE_SIZE, stride=n_heads), :].astype(BF16) for h in range(n_heads)]
        return jnp.concatenate(heads, axis=1)

    s_pages = [_dot(page_rows(k_refs[i]), w) for i in range(n_pp)]
    v_pages = [page_rows(v_refs[i]) for i in range(n_pp)]
    alpha_col, p = softmax_step(jnp.concatenate(s_pages, axis=0))
    acc_ref[...] = alpha_col * acc_ref[...] + _dot(p.T.astype(BF16), jnp.concatenate(v_pages, axis=0))

    @pl.when(g == pl.num_programs(1) - 1)
    def _():
        s_new = _dot(knew_ref[...].astype(BF16), w)
        s_row = lax.broadcasted_iota(jnp.int32, s_new.shape, 0)
        t_col = lax.rem(lax.broadcasted_iota(jnp.int32, s_new.shape, 1), l_new)
        alpha_col, p = softmax_step(jnp.where(s_row <= t_col, s_new, -jnp.inf))
        acc = alpha_col * acc_ref[...]
        for s in range(l_new):
            acc = acc + _lane_to_col(p[s:s + 1, :]) * vnew_ref[s:s + 1, :]
        acc_ref[...] = acc
        inv_l = 1.0 / _lane_to_col(l_ref[...])
        lam = lam_ref[...]
        for h in range(n_heads):
            rs = slice(h * 2 * l_new, (h + 1) * 2 * l_new)
            cs = slice(h * HEAD_W, (h + 1) * HEAD_W)
            maps = acc_ref[rs, cs] * inv_l[rs, :]
            ob = maps - lam * pltpu.roll(maps, l_new, 0)
            o_ref[:, cs] = _rms(ob, nw_ref[...]) * post_scale


def _paged_attention(q, k_new, v_new, cache_k, cache_v, page_table, lam_vec, norm_w, post_scale, n_pp):
    bsz, l_new, width = q.shape
    n_heads = width // HEAD_W
    n_pages = page_table.shape[1]
    assert 2 * l_new == SUBLANES and n_heads * 2 * l_new <= HEAD_W and n_pages % n_pp == 0
    q5 = q.reshape(bsz, l_new, n_heads, 2, DA_HEAD_DIM).transpose(0, 2, 3, 4, 1)
    sel = (jnp.eye(n_heads, dtype=BF16)[:, None, :, None] * jnp.eye(2, dtype=BF16)[None, :, None, :])
    w = q5[:, :, :, :, None, None, :] * sel[None, :, :, None, :, :, None]
    w = w.reshape(bsz, width, n_heads * 2 * l_new)
    w = jnp.pad(w, ((0, 0), (0, 0), (0, HEAD_W - w.shape[-1])))
    pad_rows = ((0, 0), (0, SUBLANES - l_new), (0, 0))
    k_new, v_new = jnp.pad(k_new, pad_rows), jnp.pad(v_new, pad_rows)

    def page_spec(i):
        return pl.BlockSpec((None, PAGE_SIZE * n_heads, HEAD_W), lambda b, g, pt: (pt[b, g * n_pp + i], 0, 0))

    new_spec = pl.BlockSpec((None, SUBLANES, width), lambda b, g, pt: (b, 0, 0))
    vec = pl.BlockSpec((1, HEAD_W), lambda b, g, pt: (0, 0))
    out = pl.pallas_call(
        functools.partial(_paged_kernel, n_pp=n_pp, n_heads=n_heads, l_new=l_new, post_scale=post_scale),
        grid_spec=pltpu.PrefetchScalarGridSpec(
            num_scalar_prefetch=1,
            grid=(bsz, n_pages // n_pp),
            in_specs=[pl.BlockSpec((None, width, HEAD_W), lambda b, g, pt: (b, 0, 0))]
            + [page_spec(i) for i in range(n_pp)] * 2 + [new_spec, new_spec, vec, vec],
            out_specs=new_spec,
            scratch_shapes=[pltpu.VMEM((1, HEAD_W), F32), pltpu.VMEM((1, HEAD_W), F32),
                            pltpu.VMEM((HEAD_W, width), F32)],
        ),
        out_shape=jax.ShapeDtypeStruct((bsz, SUBLANES, width), F32),
        compiler_params=_cparams("parallel", "arbitrary"),
        name="paged_attention",
    )(page_table, w, *([cache_k] * n_pp), *([cache_v] * n_pp), k_new, v_new, lam_vec, norm_w.reshape(1, HEAD_W))
    return out[:, :l_new, :].astype(BF16)


def _merge_kernel(oa_ref, ob_ref, wa_ref, wb_ref, ga_ref, gb_ref, u_ref, wa_bf, wb_bf):
    @pl.when(pl.program_id(1) == 0)
    def _():
        wa_bf[...] = wa_ref[...].astype(BF16)
        wb_bf[...] = wb_ref[...].astype(BF16)

    u = ga_ref[...] * _dot(oa_ref[...], wa_bf[...]) + gb_ref[...] * _dot(ob_ref[...], wb_bf[...])
    u_ref[...] = u.astype(u_ref.dtype)


def _merge(oa, ob, w_a, w_b, sig_gates, tm, tn):
    t, k = oa.shape
    n = w_a.shape[1]
    nj = n // tn
    row = pl.BlockSpec((tm, k), lambda j, i: (i, 0))
    wsp = pl.BlockSpec((k, tn), lambda j, i: (0, j))
    return pl.pallas_call(
        _merge_kernel,
        grid=(nj, t // tm),
        in_specs=[row, row, wsp, wsp, pl.BlockSpec((tm, tn), lambda j, i: (i, j)),
                  pl.BlockSpec((tm, tn), lambda j, i: (i, j + nj))],
        out_specs=pl.BlockSpec((tm, tn), lambda j, i: (i, j)),
        out_shape=jax.ShapeDtypeStruct((t, n), BF16),
        scratch_shapes=[pltpu.VMEM((k, tn), BF16), pltpu.VMEM((k, tn), BF16)],
        compiler_params=_cparams("arbitrary", "arbitrary"),
        name="branch_merge",
    )(oa, ob, w_a, w_b, sig_gates, sig_gates)


def _outproj_kernel(u_ref, w_ref, x_ref, nw_post_ref, nw_pre_ref, x1_ref, h2_ref):
    y = _dot(u_ref[...], w_ref[...])
    x1 = x_ref[...] + _rms(y, nw_post_ref[...])
    x1_ref[...] = x1
    h2_ref[...] = _rms(x1, nw_pre_ref[...]).astype(h2_ref.dtype)


def _outproj(u, w_out_bf, x, nw_post, nw_pre, tm):
    t, d = x.shape
    row = pl.BlockSpec((tm, d), lambda i: (i, 0))
    vec = pl.BlockSpec((1, d), lambda i: (0, 0))
    return pl.pallas_call(
        _outproj_kernel,
        grid=(t // tm,),
        in_specs=[row, pl.BlockSpec((d, d), lambda i: (0, 0)), row, vec, vec],
        out_specs=[row, row],
        out_shape=[jax.ShapeDtypeStruct((t, d), F32), jax.ShapeDtypeStruct((t, d), BF16)],
        compiler_params=_cparams("parallel"),
        name="out_proj",
    )(u, w_out_bf, x, nw_post.reshape(1, d), nw_pre.reshape(1, d))


def _gelu_tanh(x):
    return 0.5 * x * (1.0 + jnp.tanh(math.sqrt(2.0 / math.pi) * (x + 0.044715 * (x * x * x))))


def _ffn_up_kernel(*refs, seq_len, tiles_per_seq):
    if tiles_per_seq:
        h_ref, wg_ref, wu_ref, cw_ref, cb_ref, prev_ref, ff_ref, tail_ref, wg_bf, wu_bf, carry_ref = refs
    else:
        h_ref, wg_ref, wu_ref, cw_ref, cb_ref, halo1_ref, halo2_ref, ff_ref, g_ref, wg_bf, wu_bf = refs
    i = pl.program_id(1)

    @pl.when(i == 0)
    def _():
        wg_bf[...] = wg_ref[...].astype(BF16)
        wu_bf[...] = wu_ref[...].astype(BF16)

    if tiles_per_seq:
        @pl.when(i % tiles_per_seq == 0)
        def _():
            carry_ref[...] = prev_ref[...]

    h = h_ref[...]
    tm, tn = ff_ref.shape
    cn = min(tn, MXU_WIDTH)
    row = lax.broadcasted_iota(jnp.int32, (tm, cn), 0)
    for c in range(tn // cn):
        cs = slice(c * cn, (c + 1) * cn)
        g = _dot(h, wg_bf[:, cs])
        up = _dot(h, wu_bf[:, cs])
        g1 = pltpu.roll(g, 1, 0)
        g2 = pltpu.roll(g, 2, 0)
        if tiles_per_seq:
            p0, p1 = carry_ref[0:1, cs], carry_ref[1:2, cs]
            g1 = jnp.where(row == 0, p1, g1)
            g2 = jnp.where(row == 0, p0, jnp.where(row == 1, p1, g2))
            carry_ref[:, cs] = g[tm - 2:tm, :]
            tail_ref[:, cs] = g[tm - 2:tm, :]
        else:
            t_in_seq = lax.rem(row, seq_len)
            g1 = jnp.where(t_in_seq >= 1, g1, halo1_ref[:, cs])
            g2 = jnp.where(t_in_seq >= 2, g2, halo2_ref[:, cs])
            g_ref[:, cs] = g
        conv = cb_ref[:, cs] + g2 * cw_ref[0:1, cs] + g1 * cw_ref[1:2, cs] + g * cw_ref[2:3, cs]
        ff_ref[:, cs] = (_gelu_tanh(conv) * up).astype(ff_ref.dtype)


def _ffn_up(h2, w_gate, w_up, conv_w, conv_b, conv_prev, seq_len, tm, tn):
    t, d = h2.shape
    f = w_gate.shape[1]
    bsz = t // seq_len
    row = pl.BlockSpec((tm, d), lambda j, i: (i, 0))
    wsp = pl.BlockSpec((d, tn), lambda j, i: (0, j))
    cw = pl.BlockSpec((CONV_W, tn), lambda j, i: (0, j))
    cb = pl.BlockSpec((1, tn), lambda j, i: (0, j))
    tile = pl.BlockSpec((tm, tn), lambda j, i: (i, j))
    scratch = [pltpu.VMEM((d, tn), BF16), pltpu.VMEM((d, tn), BF16)]
    if seq_len % tm == 0:
        tps = seq_len // tm
        state = pl.BlockSpec((None, CONV_W - 1, tn), lambda j, i: (i // tps, 0, j))
        ff, tail = pl.pallas_call(
            functools.partial(_ffn_up_kernel, seq_len=seq_len, tiles_per_seq=tps),
            grid=(f // tn, t // tm),
            in_specs=[row, wsp, wsp, cw, cb, state],
            out_specs=[tile, state],
            out_shape=[jax.ShapeDtypeStruct((t, f), BF16), jax.ShapeDtypeStruct((bsz, CONV_W - 1, f), F32)],
            scratch_shapes=scratch + [pltpu.VMEM((CONV_W - 1, tn), F32)],
            compiler_params=_cparams("arbitrary", "arbitrary"),
            name="ffn_up_long",
        )(h2, w_gate, w_up, conv_w, conv_b.reshape(1, f), conv_prev)
        return ff, tail
    assert tm % seq_len == 0 and seq_len >= CONV_W - 1
    zeros = jnp.zeros((bsz, seq_len - 1, f), F32)
    halo1 = jnp.concatenate([conv_prev[:, 1:2], zeros], axis=1).reshape(t, f)
    halo2 = jnp.concatenate([conv_prev, zeros[:, 1:]], axis=1).reshape(t, f)
    ff, g = pl.pallas_call(
        functools.partial(_ffn_up_kernel, seq_len=seq_len, tiles_per_seq=0),
        grid=(f // tn, t // tm),
        in_specs=[row, wsp, wsp, cw, cb, tile, tile],
        out_specs=[tile, tile],
        out_shape=[jax.ShapeDtypeStruct((t, f), BF16), jax.ShapeDtypeStruct((t, f), F32)],
        scratch_shapes=scratch,
        compiler_params=_cparams("arbitrary", "arbitrary"),
        name="ffn_up_short",
    )(h2, w_gate, w_up, conv_w, conv_b.reshape(1, f), halo1, halo2)
    return ff, g.reshape(bsz, seq_len, f)[:, seq_len - (CONV_W - 1):, :]


def _ffn_down_kernel(ff_ref, w_ref, x_ref, nw_ref, y_ref, acc_ref):
    k = pl.program_id(1)

    @pl.when(k == 0)
    def _():
        acc_ref[...] = jnp.zeros(acc_ref.shape, F32)

    acc_ref[...] += _dot(ff_ref[...], w_ref[...])

    @pl.when(k == pl.num_programs(1) - 1)
    def _():
        y_ref[...] = x_ref[...] + _rms(acc_ref[...], nw_ref[...])


def _ffn_down(ff, w_down_bf, x1, nw, tm, tk):
    t, f = ff.shape
    d = x1.shape[1]
    row = pl.BlockSpec((tm, d), lambda i, k: (i, 0))
    return pl.pallas_call(
        _ffn_down_kernel,
        grid=(t // tm, f // tk),
        in_specs=[pl.BlockSpec((tm, tk), lambda i, k: (i, k)), pl.BlockSpec((tk, d), lambda i, k: (k, 0)),
                  row, pl.BlockSpec((1, d), lambda i, k: (0, 0))],
        out_specs=row,
        out_shape=jax.ShapeDtypeStruct((t, d), F32),
        scratch_shapes=[pltpu.VMEM((tm, d), F32)],
        compiler_params=_cparams("parallel", "arbitrary"),
        name="ffn_down",
    )(ff, w_down_bf, x1, nw.reshape(1, d))


def _tile(n, pref):
    return pref if n % pref == 0 else n


def _layer(x, pos, attend, hg_s0, conv_prev, p, lb, layer_idx):
    bsz, seq, d = x.shape
    t = bsz * seq
    hg_w = p["w_branch_a"].shape[0]
    da_w = p["w_branch_b"].shape[0]
    x2 = x.reshape(t, d)
    tm = _tile(t, 1024)

    h = _norm(x2, p["norm_mix_pre"], _tile(t, 512))
    w_in = p["w_in"]
    gates = _in_proj(h, w_in, 0, 4 * hg_w, "plain", tm, 1024)[0]
    tabs = _rotary_tables(pos)
    if seq % tm:
        tabs = tuple(jnp.tile(tb, (tm // seq, 1)) for tb in tabs)
    q_bf = _in_proj(h, w_in, 4 * hg_w, da_w, "rot_q", tm, 1024, tabs)[0]
    k_rows, k_bf = _in_proj(h, w_in, 4 * hg_w + da_w, da_w, "rot_k", tm, 1024, tabs)
    v_rows, v_bf = _in_proj(h, w_in, 4 * hg_w + 2 * da_w, da_w, "v", tm, 1024)
    sig_gates = _in_proj(h, w_in, 4 * hg_w + 3 * da_w, 2 * d, "sigmoid", tm, 1024)[0]

    gates3 = gates.reshape(bsz, seq, 4 * hg_w)
    if seq % 64 == 0:
        oa, s_new = _hgrn(gates3, lb, p["hg_norm"], hg_s0, seq, _tile(seq, 512), 64, 8)
    else:
        pad = (-seq) % SUBLANES
        gates3 = jnp.pad(gates3, ((0, 0), (0, pad), (0, 0)))
        oa, s_new = _hgrn(gates3, lb, p["hg_norm"], hg_s0, seq, seq + pad, seq + pad, hg_w // HEAD_W)
        oa = oa[:, :seq, :]
    oa = oa.reshape(t, hg_w)

    lam_init = 0.8 - 0.6 * math.exp(-0.3 * layer_idx)
    lam = p["da_lambda"].astype(F32)
    lam_full = jnp.exp(jnp.sum(lam[0] * lam[1])) - jnp.exp(jnp.sum(lam[2] * lam[3])) + lam_init
    lam_vec = jnp.full((1, HEAD_W), lam_full, F32)
    shp = (bsz, seq, da_w)
    ob = attend(q_bf.reshape(shp), k_bf.reshape(shp), v_bf.reshape(shp), k_rows, v_rows,
                lam_vec, p["da_subln"], 1.0 - lam_init).reshape(t, da_w)

    u = _merge(oa, ob, p["w_branch_a"], p["w_branch_b"], sig_gates, tm, 512)
    x1, h2 = _outproj(u, p["w_out"].astype(BF16), x2, p["norm_mix_post"], p["norm_ffn_pre"], _tile(t, 512))
    ff, conv_new = _ffn_up(h2, p["w_ffn_gate"], p["w_ffn_up"], p["conv_w"], p["conv_b"], conv_prev, seq, tm, 512)
    y = _ffn_down(ff, p["w_ffn_down"].astype(BF16), x1, p["norm_ffn_post"], _tile(t, 512), 2816)
    n_da_heads = da_w // HEAD_W
    return (y.reshape(bsz, seq, d), k_rows.reshape(bsz, seq, n_da_heads, HEAD_W),
            v_rows.reshape(bsz, seq, n_da_heads, HEAD_W), s_new, conv_new)


def kernel(x_prompt, x_sample, cache_k, cache_v, page_table, state_hgrn, state_conv, norm_mix_pre, w_in, hg_lb_logits, hg_norm, da_lambda, da_subln, w_branch_a, w_branch_b, w_out, norm_mix_post, norm_ffn_pre, w_ffn_gate, w_ffn_up, conv_w, conv_b, w_ffn_down, norm_ffn_post):
    depth = w_in.shape[0]
    lb_all = jnp.cumsum(jax.nn.softmax(hg_lb_logits.astype(F32), axis=0), axis=0)
    bp, lp = x_prompt.shape[:2]
    ls = x_sample.shape[1]
    past_len = page_table.shape[1] * PAGE_SIZE
    pos_p = jnp.arange(lp)
    pos_s = past_len + jnp.arange(ls)
    n_hg_heads = state_hgrn.shape[2]
    d_ff = state_conv.shape[-1]
    yp, ys = x_prompt, x_sample
    outs = [[] for _ in range(8)]
    for l in range(depth):
        p = dict(norm_mix_pre=norm_mix_pre[l], w_in=w_in[l], hg_norm=hg_norm[l], da_lambda=da_lambda[l],
                 da_subln=da_subln[l], w_branch_a=w_branch_a[l], w_branch_b=w_branch_b[l], w_out=w_out[l],
                 norm_mix_post=norm_mix_post[l], norm_ffn_pre=norm_ffn_pre[l], w_ffn_gate=w_ffn_gate[l],
                 w_ffn_up=w_ffn_up[l], conv_w=conv_w[l], conv_b=conv_b[l], w_ffn_down=w_ffn_down[l],
                 norm_ffn_post=norm_ffn_post[l])

        def attend_prompt(q, k, v, k_rows, v_rows, lam_vec, norm_w, post_scale):
            return _flash_attention(q, k, v, lam_vec, norm_w, post_scale, _tile(q.shape[1], 512), 1)

        def attend_sample(q, k, v, k_rows, v_rows, lam_vec, norm_w, post_scale, l=l):
            width = q.shape[-1]
            rows_per_page = PAGE_SIZE * (width // HEAD_W)
            ck = cache_k[l].reshape(cache_k.shape[1], rows_per_page, HEAD_W)
            cv = cache_v[l].reshape(cache_v.shape[1], rows_per_page, HEAD_W)
            return _paged_attention(q, k_rows.reshape(q.shape), v_rows.reshape(q.shape), ck, cv, page_table,
                                    lam_vec, norm_w, post_scale,
                                    _tile(page_table.shape[1], 8))

        yp, kp, vp, sp, cp = _layer(
            yp, pos_p, attend_prompt, jnp.zeros((bp, n_hg_heads, HEAD_W, HEAD_W), F32),
            jnp.zeros((bp, CONV_W - 1, d_ff), F32), p, lb_all[l], l)
        ys, kss, vss, sss, css = _layer(ys, pos_s, attend_sample, state_hgrn[l], state_conv[l], p, lb_all[l], l)
        for lst, val in zip(outs, (kp, vp, sp, cp, kss, vss, sss, css)):
            lst.append(val)
    kp, vp, sp, cp, kss, vss, sss, css = (jnp.stack(o) for o in outs)
    return (yp, ys, kp, vp, sp.astype(state_hgrn.dtype), cp, kss, vss, sss.astype(state_hgrn.dtype), css)
```

```python
import functools
import math

import jax
import jax.numpy as jnp
from jax import lax
from jax.experimental import pallas as pl
from jax.experimental.pallas import tpu as pltpu

F32 = jnp.float32
BF16 = jnp.bfloat16

RMS_EPS = 1e-6
ROPE_THETA = 500000.0
HEAD_W = 128
DA_HEAD_DIM = 64
ROT_DIM = DA_HEAD_DIM // 4
PAGE_SIZE = 128
CONV_W = 3
SUBLANES = 8
MXU_WIDTH = 256
VMEM_LIMIT = 56 * 1024 * 1024


def _cparams(*sem):
    return pltpu.CompilerParams(dimension_semantics=sem, vmem_limit_bytes=VMEM_LIMIT)


def _sigmoid(x):
    return 1.0 / (1.0 + jnp.exp(-x))


def _rms(x, w):
    return x * lax.rsqrt(jnp.mean(x * x, axis=-1, keepdims=True) + RMS_EPS) * w


def _dot(a, b):
    return jnp.dot(a, b, preferred_element_type=F32)


def _dot_nt(a, b):
    return lax.dot_general(a, b, (((1,), (1,)), ((), ())), preferred_element_type=F32)


def _dot_tn(a, b):
    return lax.dot_general(a, b, (((0,), (0,)), ((), ())), preferred_element_type=F32)


def _norm_kernel(x_ref, w_ref, o_ref):
    o_ref[...] = _rms(x_ref[...], w_ref[...]).astype(o_ref.dtype)


def _norm(x, w, tm):
    t, d = x.shape
    return pl.pallas_call(
        _norm_kernel,
        grid=(t // tm,),
        in_specs=[pl.BlockSpec((tm, d), lambda i: (i, 0)), pl.BlockSpec((1, d), lambda i: (0, 0))],
        out_specs=pl.BlockSpec((tm, d), lambda i: (i, 0)),
        out_shape=jax.ShapeDtypeStruct((t, d), BF16),
        compiler_params=_cparams("parallel"),
        name="pre_norm",
    )(x, w.reshape(1, d))


def _rotary_tile(a, cos, sin_lo, sin_hi):
    return a * cos + pltpu.roll(a, ROT_DIM // 2, 1) * sin_hi + pltpu.roll(a, HEAD_W - ROT_DIM // 2, 1) * sin_lo


def _proj_kernel(*refs, mode, n_tab):
    h_ref, w_ref = refs[0], refs[1]
    tabs = refs[2:2 + n_tab]
    outs = refs[2 + n_tab:-1]
    wbf_ref = refs[-1]

    @pl.when(pl.program_id(1) == 0)
    def _():
        wbf_ref[...] = w_ref[...].astype(BF16)

    acc = _dot(h_ref[...], wbf_ref[...])
    n_grp = acc.shape[1] // HEAD_W
    if mode == "plain":
        outs[0][...] = acc
    elif mode == "sigmoid":
        outs[0][...] = _sigmoid(acc).astype(BF16)
    elif mode == "v":
        for g in range(acc.shape[1] // HEAD_W):
            outs[0][pl.ds(g, acc.shape[0], stride=n_grp), :] = acc[:, g * HEAD_W:(g + 1) * HEAD_W]
        outs[1][...] = acc.astype(BF16)
    else:
        cos, sin_lo, sin_hi = (t[...] for t in tabs)
        for g in range(acc.shape[1] // HEAD_W):
            sl = slice(g * HEAD_W, (g + 1) * HEAD_W)
            r = _rotary_tile(acc[:, sl], cos, sin_lo, sin_hi)
            if mode == "rot_q":
                outs[0][:, sl] = (r * (DA_HEAD_DIM ** -0.5)).astype(BF16)
            else:
                outs[0][pl.ds(g, acc.shape[0], stride=n_grp), :] = r
                outs[1][:, sl] = r.astype(BF16)


def _in_proj(h, w_in, col_off, n_cols, mode, tm, tn, tables=()):
    t, k = h.shape
    out_dtypes = {"plain": (F32,), "sigmoid": (BF16,), "v": (F32, BF16), "rot_q": (BF16,), "rot_k": (F32, BF16)}[mode]
    out_specs = [pl.BlockSpec((tm, tn), lambda j, i: (i, j)) for _ in out_dtypes]
    out_shape = [jax.ShapeDtypeStruct((t, n_cols), dt) for dt in out_dtypes]
    if mode in ("v", "rot_k"):
        assert n_cols == tn
        n_grp = tn // HEAD_W
        out_specs[0] = pl.BlockSpec((tm * n_grp, HEAD_W), lambda j, i: (i, 0))
        out_shape[0] = jax.ShapeDtypeStruct((t * n_grp, HEAD_W), F32)
    blk_off = col_off // tn
    in_specs = [pl.BlockSpec((tm, k), lambda j, i: (i, 0)),
                pl.BlockSpec((k, tn), lambda j, i: (0, j + blk_off))]
    tab_tiles = tables[0].shape[0] // tm if tables else 1
    in_specs += [pl.BlockSpec((tm, HEAD_W), lambda j, i: (i % tab_tiles, 0)) for _ in tables]
    outs = pl.pallas_call(
        functools.partial(_proj_kernel, mode=mode, n_tab=len(tables)),
        grid=(n_cols // tn, t // tm),
        in_specs=in_specs,
        out_specs=out_specs,
        out_shape=out_shape,
        scratch_shapes=[pltpu.VMEM((k, tn), BF16)],
        compiler_params=_cparams("arbitrary", "arbitrary"),
        name="in_proj_" + mode,
    )(h, w_in, *tables)
    return outs


def _rotary_tables(pos):
    half = ROT_DIM // 2
    inv_freq = jnp.power(jnp.float32(ROPE_THETA), -jnp.arange(half, dtype=F32) * (2.0 / ROT_DIM))
    ang = pos.astype(F32)[:, None] * inv_freq[None, :]
    cos, sin = jnp.cos(ang), jnp.sin(ang)
    lane = jnp.arange(HEAD_W) % DA_HEAD_DIM
    idx = lane % half
    is_lo, is_hi = lane < half, (lane >= half) & (lane < ROT_DIM)
    cos_t = jnp.where((is_lo | is_hi)[None, :], cos[:, idx], 1.0)
    sin_lo = jnp.where(is_lo[None, :], -sin[:, idx], 0.0)
    sin_hi = jnp.where(is_hi[None, :], sin[:, idx], 0.0)
    return cos_t, sin_lo, sin_hi


def _hgrn_kernel(hq_ref, hf_ref, hi_ref, hg_ref, lb_ref, nw_ref, s0_ref, o_ref, sout_ref,
                 st_ref, b_ref, k_ref, v_ref, *, n_heads, chunk, n_chunks, l_valid):
    c_len = chunk
    step = pl.program_id(2)

    @pl.when(step == 0)
    def _():
        for hh in range(n_heads):
            st_ref[hh] = s0_ref[hh].T

    row = lax.broadcasted_iota(jnp.int32, (c_len, c_len), 0)
    col = lax.broadcasted_iota(jnp.int32, (c_len, c_len), 1)
    tri = jnp.where(col <= row, 1.0, 0.0).astype(BF16)
    level_masks = {}
    h = SUBLANES
    while h < c_len:
        same_blk = jnp.bitwise_and(jnp.bitwise_xor(row, col), -2 * h) == 0
        level_masks[h] = same_blk & (jnp.bitwise_and(row, h) != 0) & (jnp.bitwise_and(col, h) == 0)
        h *= 2
    sub_t = lax.broadcasted_iota(jnp.int32, (SUBLANES, HEAD_W), 0)

    def head_chunk(hh, r0):
        rows = pl.ds(r0, c_len)
        lanes = slice(hh * HEAD_W, (hh + 1) * HEAD_W)
        lb = lb_ref[:, lanes]
        hq, hf, hi, hg = hq_ref[rows, lanes], hf_ref[rows, lanes], hi_ref[rows, lanes], hg_ref[rows, lanes]
        f = lb + (1.0 - lb) * _sigmoid(hf)
        logf = jnp.log(f)
        kk = 1.0 - f
        if l_valid < c_len * n_chunks:
            valid = (lax.broadcasted_iota(jnp.int32, (c_len, HEAD_W), 0) + r0) < l_valid
            logf = jnp.where(valid, logf, 0.0)
            kk = jnp.where(valid, kk, 0.0)
        q = hq * _sigmoid(hq)
        p_hi = logf.astype(BF16)
        rem = logf - p_hi.astype(F32)
        p_mid = rem.astype(BF16)
        p_lo = (rem - p_mid.astype(F32)).astype(BF16)
        csum = _dot(tri, jnp.concatenate([p_hi, p_mid, p_lo], axis=1))
        b = csum[:, 0:HEAD_W] + csum[:, HEAD_W:2 * HEAD_W] + csum[:, 2 * HEAD_W:3 * HEAD_W]
        b_ref[hh] = b
        k_ref[hh] = kk
        v_ref[hh] = hi
        b_last = b[c_len - 1:c_len, :]

        diag_blocks = []
        for blk in range(c_len // SUBLANES):
            rs = slice(blk * SUBLANES, (blk + 1) * SUBLANES)
            q_blk, b_blk = q[rs, :], b[rs, :]
            od = jnp.zeros((SUBLANES, HEAD_W), F32)
            for s in range(SUBLANES):
                r = blk * SUBLANES + s
                b_s = b_ref[hh, pl.ds(r, 1), :]
                k_s = k_ref[hh, pl.ds(r, 1), :]
                v_s = v_ref[hh, pl.ds(r, 1), :]
                e = jnp.where(sub_t >= s, jnp.exp(b_blk - b_s), 0.0)
                a_col = jnp.sum(q_blk * k_s * e, axis=1, keepdims=True)
                od = od + a_col * v_s
            diag_blocks.append(od)
        o = jnp.concatenate(diag_blocks, axis=0) if len(diag_blocks) > 1 else diag_blocks[0]

        a_off = None
        h = SUBLANES
        while h < c_len:
            grp = 2 * h
            pieces = [jnp.broadcast_to(b_ref[hh, pl.ds(gi * grp + h - 1, 1), :], (grp, HEAD_W))
                      for gi in range(c_len // grp)]
            ref = jnp.concatenate(pieces, axis=0) if len(pieces) > 1 else pieces[0]
            q_h = q * jnp.exp(jnp.minimum(b - ref, 0.0))
            k_h = kk * jnp.exp(jnp.minimum(ref - b, 0.0))
            a_h = _dot_nt(q_h.astype(BF16), k_h.astype(BF16))
            a_h = jnp.where(level_masks[h], a_h, 0.0)
            a_off = a_h if a_off is None else a_off + a_h
            h = grp
        v_bf = hi.astype(BF16)
        if a_off is not None:
            o = o + _dot(a_off.astype(BF16), v_bf)

        st = st_ref[hh]
        o = o + _dot_nt((q * jnp.exp(b)).astype(BF16), st.astype(BF16))
        k_dec = kk * jnp.exp(b_last - b)
        st_ref[hh] = st * jnp.exp(b_last) + _dot_tn(v_bf, k_dec.astype(BF16))

        o_ref[rows, lanes] = (_rms(o, nw_ref[...]) * (hg * _sigmoid(hg))).astype(o_ref.dtype)

    def one_chunk(c, carry):
        r0 = pl.multiple_of(c * c_len, c_len)
        for hh in range(n_heads):
            head_chunk(hh, r0)
        return carry

    lax.fori_loop(0, n_chunks, one_chunk, 0)

    @pl.when(step == pl.num_programs(2) - 1)
    def _():
        for hh in range(n_heads):
            sout_ref[hh] = st_ref[hh].T


def _hgrn(gates, lb, norm_w, s0, l_valid, block_len, chunk, heads_per_step):
    bsz, seq, width = gates.shape
    n_heads = width // (4 * HEAD_W)
    hb = heads_per_step
    n_grp = n_heads // hb

    def gate_spec(g):
        return pl.BlockSpec((None, block_len, hb * HEAD_W), lambda b, h, s: (b, s, g * n_grp + h))

    head_vec = pl.BlockSpec((1, hb * HEAD_W), lambda b, h, s: (0, h))
    state_spec = pl.BlockSpec((None, hb, HEAD_W, HEAD_W), lambda b, h, s: (b, h, 0, 0))
    return pl.pallas_call(
        functools.partial(_hgrn_kernel, n_heads=hb, chunk=chunk, n_chunks=block_len // chunk, l_valid=l_valid),
        grid=(bsz, n_grp, seq // block_len),
        in_specs=[gate_spec(0), gate_spec(1), gate_spec(2), gate_spec(3), head_vec,
                  pl.BlockSpec((1, HEAD_W), lambda b, h, s: (0, 0)), state_spec],
        out_specs=[pl.BlockSpec((None, block_len, hb * HEAD_W), lambda b, h, s: (b, s, h)), state_spec],
        out_shape=[jax.ShapeDtypeStruct((bsz, seq, n_heads * HEAD_W), BF16),
                   jax.ShapeDtypeStruct(s0.shape, F32)],
        scratch_shapes=[pltpu.VMEM((hb, HEAD_W, HEAD_W), F32)] + [pltpu.VMEM((hb, chunk, HEAD_W), F32)] * 3,
        compiler_params=_cparams("parallel", "parallel", "arbitrary"),
        name="hgrn",
    )(gates, gates, gates, gates, lb.reshape(1, -1), norm_w.reshape(1, HEAD_W), s0)


def _flash_kernel(q_ref, k_ref, v_ref, lam_ref, nw_ref, o_ref, qs_ref, sa_ref, sb_ref, m_ref, acc_ref,
                  *, tq, n_heads, post_scale):
    qi = pl.program_id(2)
    n_rows = 2 * tq
    lane = lax.broadcasted_iota(jnp.int32, (tq, HEAD_W), 1)
    zero = jnp.zeros((tq, HEAD_W), BF16)
    for hh in range(n_heads):
        q = q_ref[:, hh * HEAD_W:(hh + 1) * HEAD_W]
        qs_ref[hh, 0:tq, :] = jnp.where(lane < DA_HEAD_DIM, q, zero)
        qs_ref[hh, tq:n_rows, :] = jnp.where(lane >= DA_HEAD_DIM, q, zero)
    m_ref[...] = jnp.full(m_ref.shape, -jnp.inf, F32)
    acc_ref[...] = jnp.zeros(acc_ref.shape, F32)
    lane_reps = tq // HEAD_W
    ones_blk = jnp.ones((tq, HEAD_W), BF16)

    def tile_rows(t):
        return pl.ds(pl.multiple_of(t * tq, tq), tq)

    def scores(t, s_ref):
        for hh in range(n_heads):
            s_ref[hh] = _dot_nt(qs_ref[hh], k_ref[tile_rows(t), hh * HEAD_W:(hh + 1) * HEAD_W])

    def consume(t, s_ref, masked):
        for hh in range(n_heads):
            if masked:
                r = lax.broadcasted_iota(jnp.int32, (n_rows, tq), 0)
                c = lax.broadcasted_iota(jnp.int32, (n_rows, tq), 1)
                s_ref[hh] = jnp.where(c <= jnp.where(r >= tq, r - tq, r), s_ref[hh], -jnp.inf)
            part = s_ref[hh, :, 0:HEAD_W]
            for j in range(1, lane_reps):
                part = jnp.maximum(part, s_ref[hh, :, j * HEAD_W:(j + 1) * HEAD_W])
            m_old = m_ref[hh]
            m_new = jnp.maximum(m_old, jnp.max(part, axis=1, keepdims=True))
            alpha = jnp.exp(m_old - m_new)
            m_ref[hh] = m_new
            p = jnp.exp(s_ref[hh] - jnp.concatenate([m_new] * lane_reps, axis=1)).astype(BF16)
            v = v_ref[tile_rows(t), hh * HEAD_W:(hh + 1) * HEAD_W]
            pv = _dot(p, jnp.concatenate([v, ones_blk], axis=1))
            acc_ref[hh] = jnp.concatenate([alpha, alpha], axis=1) * acc_ref[hh] + pv

    scores(0, sa_ref)

    def pair(j, carry):
        t = 2 * j
        scores(t + 1, sb_ref)
        consume(t, sa_ref, False)
        scores(t + 2, sa_ref)
        consume(t + 1, sb_ref, False)
        return carry

    lax.fori_loop(0, lax.shift_right_logical(qi, 1), pair, 0)
    odd = lax.bitwise_and(qi, 1)

    @pl.when(odd == 1)
    def _():
        scores(qi, sb_ref)
        consume(qi - 1, sa_ref, False)
        consume(qi, sb_ref, True)

    @pl.when(odd == 0)
    def _():
        consume(qi, sa_ref, True)

    for hh in range(n_heads):
        out = acc_ref[hh, :, 0:HEAD_W] / acc_ref[hh, :, HEAD_W:2 * HEAD_W]
        ob = out[0:tq, :] - lam_ref[...] * out[tq:n_rows, :]
        o_ref[:, hh * HEAD_W:(hh + 1) * HEAD_W] = (_rms(ob, nw_ref[...]) * post_scale).astype(o_ref.dtype)


def _flash_attention(q, k, v, lam_vec, norm_w, post_scale, tq, heads_per_step):
    bsz, seq, width = q.shape
    hb = heads_per_step
    kv_spec = pl.BlockSpec((None, seq, hb * HEAD_W), lambda b, h, i: (b, 0, h))
    q_spec = pl.BlockSpec((None, tq, hb * HEAD_W), lambda b, h, i: (b, i, h))
    vec = pl.BlockSpec((1, HEAD_W), lambda b, h, i: (0, 0))
    return pl.pallas_call(
        functools.partial(_flash_kernel, tq=tq, n_heads=hb, post_scale=post_scale),
        grid=(bsz, width // (hb * HEAD_W), seq // tq),
        in_specs=[q_spec, kv_spec, kv_spec, vec, vec],
        out_specs=q_spec,
        out_shape=jax.ShapeDtypeStruct((bsz, seq, width), BF16),
        scratch_shapes=[pltpu.VMEM((hb, 2 * tq, HEAD_W), BF16), pltpu.VMEM((hb, 2 * tq, tq), F32),
                        pltpu.VMEM((hb, 2 * tq, tq), F32), pltpu.VMEM((hb, 2 * tq, HEAD_W), F32),
                        pltpu.VMEM((hb, 2 * tq, 2 * HEAD_W), F32)],
        compiler_params=_cparams("parallel", "parallel", "arbitrary"),
        name="flash_attention",
    )(q, k, v, lam_vec, norm_w.reshape(1, HEAD_W))


def _lane_to_col(vec):
    n = vec.shape[1]
    r = lax.broadcasted_iota(jnp.int32, (n, n), 0)
    c = lax.broadcasted_iota(jnp.int32, (n, n), 1)
    return jnp.sum(jnp.where(r == c, jnp.broadcast_to(vec, (n, n)), 0.0), axis=1, keepdims=True)


def _paged_kernel(pt_ref, q_ref, k_hbm, v_hbm, knew_ref, vnew_ref, lam_ref, nw_ref, o_ref,
                  wt_ref, m_ref, l_ref, acc_ref, kbuf, vbuf, sem, *, n_pp, n_heads, l_new, n_slots, post_scale):
    n_g = pl.num_programs(1)
    g = pl.program_id(1)
    step = pl.program_id(0) * n_g + g
    n_steps = pl.num_programs(0) * n_g

    def page_copies(s, slot):
        sb = lax.div(s, n_g)
        first = (s - sb * n_g) * n_pp
        copies = []
        for i in range(n_pp):
            page = pt_ref[sb, first + i]
            copies.append(pltpu.make_async_copy(k_hbm.at[page], kbuf.at[slot, i], sem.at[slot, 0]))
            copies.append(pltpu.make_async_copy(v_hbm.at[page], vbuf.at[slot, i], sem.at[slot, 1]))
        return copies

    @pl.when(step == 0)
    def _():
        for s in range(n_slots - 1):
            for cp in page_copies(s, s):
                cp.start()

    ahead = step + (n_slots - 1)

    @pl.when(ahead < n_steps)
    def _():
        for cp in page_copies(ahead, lax.rem(ahead, n_slots)):
            cp.start()

    @pl.when(g == 0)
    def _():
        m_ref[...] = jnp.full(m_ref.shape, -jnp.inf, F32)
        l_ref[...] = jnp.zeros(l_ref.shape, F32)
        acc_ref[...] = jnp.zeros(acc_ref.shape, F32)
        q_rows = jnp.concatenate([q_ref[...].astype(F32)] * (SUBLANES // l_new), axis=0)
        q_rep = jnp.concatenate([q_rows] * (HEAD_W // SUBLANES), axis=0)
        col = lax.broadcasted_iota(jnp.int32, q_rep.shape, 0)
        lane = lax.broadcasted_iota(jnp.int32, q_rep.shape, 1)
        own = (lax.div(lane, DA_HEAD_DIM) == lax.div(col, l_new)) & (col < n_heads * 2 * l_new)
        wt_ref[...] = jnp.where(own, q_rep, 0.0).astype(BF16)

    slot = lax.rem(step, n_slots)
    for cp in page_copies(step, slot):
        cp.wait()
    w_t = wt_ref[...]

    def softmax_step(s_t):
        m_old = m_ref[...]
        m_new = jnp.maximum(m_old, jnp.max(s_t, axis=0, keepdims=True))
        alpha = jnp.exp(m_old - m_new)
        p = jnp.exp(s_t - m_new)
        l_ref[...] = alpha * l_ref[...] + jnp.sum(p, axis=0, keepdims=True)
        m_ref[...] = m_new
        return _lane_to_col(alpha), p

    def page_rows(ref):
        heads = [ref[pl.ds(h, PAGE_SIZE, stride=n_heads), :].astype(BF16) for h in range(n_heads)]
        return jnp.concatenate(heads, axis=1)

    s_pages = [_dot_nt(page_rows(kbuf.at[slot, i]), w_t) for i in range(n_pp)]
    v_pages = [page_rows(vbuf.at[slot, i]) for i in range(n_pp)]
    alpha_col, p = softmax_step(jnp.concatenate(s_pages, axis=0))
    acc_ref[...] = alpha_col * acc_ref[...] + _dot(p.T.astype(BF16), jnp.concatenate(v_pages, axis=0))

    @pl.when(g == pl.num_programs(1) - 1)
    def _():
        s_new = _dot_nt(knew_ref[...].astype(BF16), w_t)
        s_row = lax.broadcasted_iota(jnp.int32, s_new.shape, 0)
        t_col = lax.rem(lax.broadcasted_iota(jnp.int32, s_new.shape, 1), l_new)
        alpha_col, p = softmax_step(jnp.where(s_row <= t_col, s_new, -jnp.inf))
        acc = alpha_col * acc_ref[...]
        for s in range(l_new):
            acc = acc + _lane_to_col(p[s:s + 1, :]) * vnew_ref[s:s + 1, :]
        acc_ref[...] = acc
        inv_l = 1.0 / _lane_to_col(l_ref[...])
        lam = lam_ref[...]
        for h in range(n_heads):
            rs = slice(h * 2 * l_new, (h + 1) * 2 * l_new)
            cs = slice(h * HEAD_W, (h + 1) * HEAD_W)
            maps = acc_ref[rs, cs] * inv_l[rs, :]
            ob = maps - lam * pltpu.roll(maps, l_new, 0)
            o_ref[:, cs] = _rms(ob, nw_ref[...]) * post_scale


def _paged_attention(q, k_new, v_new, cache_k, cache_v, page_table, lam_vec, norm_w, post_scale, n_pp):
    bsz, l_new, width = q.shape
    n_heads = width // HEAD_W
    n_pages = page_table.shape[1]
    assert 2 * l_new == SUBLANES and n_heads * 2 * l_new <= HEAD_W and n_pages % n_pp == 0
    pad_rows = ((0, 0), (0, SUBLANES - l_new), (0, 0))
    k_new, v_new = jnp.pad(k_new, pad_rows), jnp.pad(v_new, pad_rows)
    n_slots = 3
    page_shape = (PAGE_SIZE * n_heads, HEAD_W)
    new_spec = pl.BlockSpec((None, SUBLANES, width), lambda b, g, pt: (b, 0, 0))
    vec = pl.BlockSpec((1, HEAD_W), lambda b, g, pt: (0, 0))
    hbm = pl.BlockSpec(memory_space=pl.ANY)
    out = pl.pallas_call(
        functools.partial(_paged_kernel, n_pp=n_pp, n_heads=n_heads, l_new=l_new, n_slots=n_slots,
                          post_scale=post_scale),
        grid_spec=pltpu.PrefetchScalarGridSpec(
            num_scalar_prefetch=1,
            grid=(bsz, n_pages // n_pp),
            in_specs=[pl.BlockSpec((None, l_new, width), lambda b, g, pt: (b, 0, 0)), hbm, hbm,
                      new_spec, new_spec, vec, vec],
            out_specs=new_spec,
            scratch_shapes=[pltpu.VMEM((HEAD_W, width), BF16), pltpu.VMEM((1, HEAD_W), F32),
                            pltpu.VMEM((1, HEAD_W), F32), pltpu.VMEM((HEAD_W, width), F32),
                            pltpu.VMEM((n_slots, n_pp) + page_shape, F32),
                            pltpu.VMEM((n_slots, n_pp) + page_shape, F32),
                            pltpu.SemaphoreType.DMA((n_slots, 2))],
        ),
        out_shape=jax.ShapeDtypeStruct((bsz, SUBLANES, width), F32),
        compiler_params=_cparams("arbitrary", "arbitrary"),
        name="paged_attention",
    )(page_table, q, cache_k, cache_v, k_new, v_new, lam_vec, norm_w.reshape(1, HEAD_W))
    return out[:, :l_new, :].astype(BF16)


def _merge_kernel(oa_ref, ob_ref, wa_ref, wb_ref, ga_ref, gb_ref, u_ref, wa_bf, wb_bf):
    @pl.when(pl.program_id(1) == 0)
    def _():
        wa_bf[...] = wa_ref[...].astype(BF16)
        wb_bf[...] = wb_ref[...].astype(BF16)

    u = ga_ref[...] * _dot(oa_ref[...], wa_bf[...]) + gb_ref[...] * _dot(ob_ref[...], wb_bf[...])
    u_ref[...] = u.astype(u_ref.dtype)


def _merge(oa, ob, w_a, w_b, sig_gates, tm, tn):
    t, k = oa.shape
    n = w_a.shape[1]
    nj = n // tn
    row = pl.BlockSpec((tm, k), lambda j, i: (i, 0))
    wsp = pl.BlockSpec((k, tn), lambda j, i: (0, j))
    return pl.pallas_call(
        _merge_kernel,
        grid=(nj, t // tm),
        in_specs=[row, row, wsp, wsp, pl.BlockSpec((tm, tn), lambda j, i: (i, j)),
                  pl.BlockSpec((tm, tn), lambda j, i: (i, j + nj))],
        out_specs=pl.BlockSpec((tm, tn), lambda j, i: (i, j)),
        out_shape=jax.ShapeDtypeStruct((t, n), BF16),
        scratch_shapes=[pltpu.VMEM((k, tn), BF16), pltpu.VMEM((k, tn), BF16)],
        compiler_params=_cparams("arbitrary", "arbitrary"),
        name="branch_merge",
    )(oa, ob, w_a, w_b, sig_gates, sig_gates)


def _outproj_kernel(u_ref, w_ref, x_ref, nw_post_ref, nw_pre_ref, x1_ref, h2_ref):
    y = _dot(u_ref[...], w_ref[...])
    x1 = x_ref[...] + _rms(y, nw_post_ref[...])
    x1_ref[...] = x1
    h2_ref[...] = _rms(x1, nw_pre_ref[...]).astype(h2_ref.dtype)


def _outproj(u, w_out_bf, x, nw_post, nw_pre, tm):
    t, d = x.shape
    row = pl.BlockSpec((tm, d), lambda i: (i, 0))
    vec = pl.BlockSpec((1, d), lambda i: (0, 0))
    return pl.pallas_call(
        _outproj_kernel,
        grid=(t // tm,),
        in_specs=[row, pl.BlockSpec((d, d), lambda i: (0, 0)), row, vec, vec],
        out_specs=[row, row],
        out_shape=[jax.ShapeDtypeStruct((t, d), F32), jax.ShapeDtypeStruct((t, d), BF16)],
        compiler_params=_cparams("parallel"),
        name="out_proj",
    )(u, w_out_bf, x, nw_post.reshape(1, d), nw_pre.reshape(1, d))


def _gelu_tanh(x):
    return 0.5 * x * (1.0 + jnp.tanh(math.sqrt(2.0 / math.pi) * (x + 0.044715 * (x * x * x))))


def _ffn_up_kernel(*refs, seq_len, tiles_per_seq):
    if tiles_per_seq:
        h_ref, wg_ref, wu_ref, cw_ref, cb_ref, prev_ref, ff_ref, tail_ref, wg_bf, wu_bf, carry_ref = refs
    else:
        h_ref, wg_ref, wu_ref, cw_ref, cb_ref, halo1_ref, halo2_ref, ff_ref, g_ref, wg_bf, wu_bf = refs
    i = pl.program_id(1)

    @pl.when(i == 0)
    def _():
        wg_bf[...] = wg_ref[...].astype(BF16)
        wu_bf[...] = wu_ref[...].astype(BF16)

    if tiles_per_seq:
        @pl.when(i % tiles_per_seq == 0)
        def _():
            carry_ref[...] = prev_ref[...]

    h = h_ref[...]
    tm, tn = ff_ref.shape
    cn = min(tn, MXU_WIDTH)
    row = lax.broadcasted_iota(jnp.int32, (tm, cn), 0)
    for c in range(tn // cn):
        cs = slice(c * cn, (c + 1) * cn)
        g = _dot(h, wg_bf[:, cs])
        up = _dot(h, wu_bf[:, cs])
        g1 = pltpu.roll(g, 1, 0)
        g2 = pltpu.roll(g, 2, 0)
        if tiles_per_seq:
            p0, p1 = carry_ref[0:1, cs], carry_ref[1:2, cs]
            g1 = jnp.where(row == 0, p1, g1)
            g2 = jnp.where(row == 0, p0, jnp.where(row == 1, p1, g2))
            carry_ref[:, cs] = g[tm - 2:tm, :]
            tail_ref[:, cs] = g[tm - 2:tm, :]
        else:
            t_in_seq = lax.rem(row, seq_len)
            g1 = jnp.where(t_in_seq >= 1, g1, halo1_ref[:, cs])
            g2 = jnp.where(t_in_seq >= 2, g2, halo2_ref[:, cs])
            g_ref[:, cs] = g
        conv = cb_ref[:, cs] + g2 * cw_ref[0:1, cs] + g1 * cw_ref[1:2, cs] + g * cw_ref[2:3, cs]
        ff_ref[:, cs] = (_gelu_tanh(conv) * up).astype(ff_ref.dtype)


def _ffn_up(h2, w_gate, w_up, conv_w, conv_b, conv_prev, seq_len, tm, tn):
    t, d = h2.shape
    f = w_gate.shape[1]
    bsz = t // seq_len
    row = pl.BlockSpec((tm, d), lambda j, i: (i, 0))
    wsp = pl.BlockSpec((d, tn), lambda j, i: (0, j))
    cw = pl.BlockSpec((CONV_W, tn), lambda j, i: (0, j))
    cb = pl.BlockSpec((1, tn), lambda j, i: (0, j))
    tile = pl.BlockSpec((tm, tn), lambda j, i: (i, j))
    scratch = [pltpu.VMEM((d, tn), BF16), pltpu.VMEM((d, tn), BF16)]
    if seq_len % tm == 0:
        tps = seq_len // tm
        state = pl.BlockSpec((None, CONV_W - 1, tn), lambda j, i: (i // tps, 0, j))
        ff, tail = pl.pallas_call(
            functools.partial(_ffn_up_kernel, seq_len=seq_len, tiles_per_seq=tps),
            grid=(f // tn, t // tm),
            in_specs=[row, wsp, wsp, cw, cb, state],
            out_specs=[tile, state],
            out_shape=[jax.ShapeDtypeStruct((t, f), BF16), jax.ShapeDtypeStruct((bsz, CONV_W - 1, f), F32)],
            scratch_shapes=scratch + [pltpu.VMEM((CONV_W - 1, tn), F32)],
            compiler_params=_cparams("arbitrary", "arbitrary"),
            name="ffn_up_long",
        )(h2, w_gate, w_up, conv_w, conv_b.reshape(1, f), conv_prev)
        return ff, tail
    assert tm % seq_len == 0 and seq_len >= CONV_W - 1
    zeros = jnp.zeros((bsz, seq_len - 1, f), F32)
    halo1 = jnp.concatenate([conv_prev[:, 1:2], zeros], axis=1).reshape(t, f)
    halo2 = jnp.concatenate([conv_prev, zeros[:, 1:]], axis=1).reshape(t, f)
    ff, g = pl.pallas_call(
        functools.partial(_ffn_up_kernel, seq_len=seq_len, tiles_per_seq=0),
        grid=(f // tn, t // tm),
        in_specs=[row, wsp, wsp, cw, cb, tile, tile],
        out_specs=[tile, tile],
        out_shape=[jax.ShapeDtypeStruct((t, f), BF16), jax.ShapeDtypeStruct((t, f), F32)],
        scratch_shapes=scratch,
        compiler_params=_cparams("arbitrary", "arbitrary"),
        name="ffn_up_short",
    )(h2, w_gate, w_up, conv_w, conv_b.reshape(1, f), halo1, halo2)
    return ff, g.reshape(bsz, seq_len, f)[:, seq_len - (CONV_W - 1):, :]


def _ffn_down_kernel(ff_ref, w_ref, x_ref, nw_ref, y_ref, acc_ref):
    k = pl.program_id(1)

    @pl.when(k == 0)
    def _():
        acc_ref[...] = jnp.zeros(acc_ref.shape, F32)

    acc_ref[...] += _dot(ff_ref[...], w_ref[...])

    @pl.when(k == pl.num_programs(1) - 1)
    def _():
        y_ref[...] = x_ref[...] + _rms(acc_ref[...], nw_ref[...])


def _ffn_down(ff, w_down_bf, x1, nw, tm, tk):
    t, f = ff.shape
    d = x1.shape[1]
    row = pl.BlockSpec((tm, d), lambda i, k: (i, 0))
    return pl.pallas_call(
        _ffn_down_kernel,
        grid=(t // tm, f // tk),
        in_specs=[pl.BlockSpec((tm, tk), lambda i, k: (i, k)), pl.BlockSpec((tk, d), lambda i, k: (k, 0)),
                  row, pl.BlockSpec((1, d), lambda i, k: (0, 0))],
        out_specs=row,
        out_shape=jax.ShapeDtypeStruct((t, d), F32),
        scratch_shapes=[pltpu.VMEM((tm, d), F32)],
        compiler_params=_cparams("parallel", "arbitrary"),
        name="ffn_down",
    )(ff, w_down_bf, x1, nw.reshape(1, d))


def _tile(n, pref):
    return pref if n % pref == 0 else n


def _layer(x, pos, attend, hg_s0, conv_prev, p, lb, layer_idx):
    bsz, seq, d = x.shape
    t = bsz * seq
    hg_w = p["w_branch_a"].shape[0]
    da_w = p["w_branch_b"].shape[0]
    x2 = x.reshape(t, d)
    tm = _tile(t, 1024)

    h = _norm(x2, p["norm_mix_pre"], _tile(t, 512))
    w_in = p["w_in"]
    gates = _in_proj(h, w_in, 0, 4 * hg_w, "plain", tm, 1024)[0]
    tabs = _rotary_tables(pos)
    if seq % tm:
        tabs = tuple(jnp.tile(tb, (tm // seq, 1)) for tb in tabs)
    q_bf = _in_proj(h, w_in, 4 * hg_w, da_w, "rot_q", tm, 1024, tabs)[0]
    k_rows, k_bf = _in_proj(h, w_in, 4 * hg_w + da_w, da_w, "rot_k", tm, 1024, tabs)
    v_rows, v_bf = _in_proj(h, w_in, 4 * hg_w + 2 * da_w, da_w, "v", tm, 1024)
    sig_gates = _in_proj(h, w_in, 4 * hg_w + 3 * da_w, 2 * d, "sigmoid", tm, 1024)[0]

    gates3 = gates.reshape(bsz, seq, 4 * hg_w)
    if seq % 64 == 0:
        oa, s_new = _hgrn(gates3, lb, p["hg_norm"], hg_s0, seq, _tile(seq, 512), 64, 8)
    else:
        pad = (-seq) % SUBLANES
        gates3 = jnp.pad(gates3, ((0, 0), (0, pad), (0, 0)))
        oa, s_new = _hgrn(gates3, lb, p["hg_norm"], hg_s0, seq, seq + pad, seq + pad, hg_w // HEAD_W)
        oa = oa[:, :seq, :]
    oa = oa.reshape(t, hg_w)

    lam_init = 0.8 - 0.6 * math.exp(-0.3 * layer_idx)
    lam = p["da_lambda"].astype(F32)
    lam_full = jnp.exp(jnp.sum(lam[0] * lam[1])) - jnp.exp(jnp.sum(lam[2] * lam[3])) + lam_init
    lam_vec = jnp.full((1, HEAD_W), lam_full, F32)
    shp = (bsz, seq, da_w)
    ob = attend(q_bf.reshape(shp), k_bf.reshape(shp), v_bf.reshape(shp), k_rows, v_rows,
                lam_vec, p["da_subln"], 1.0 - lam_init).reshape(t, da_w)

    u = _merge(oa, ob, p["w_branch_a"], p["w_branch_b"], sig_gates, tm, 512)
    x1, h2 = _outproj(u, p["w_out"].astype(BF16), x2, p["norm_mix_post"], p["norm_ffn_pre"], _tile(t, 512))
    ff, conv_new = _ffn_up(h2, p["w_ffn_gate"], p["w_ffn_up"], p["conv_w"], p["conv_b"], conv_prev, seq, tm, 512)
    y = _ffn_down(ff, p["w_ffn_down"].astype(BF16), x1, p["norm_ffn_post"], _tile(t, 512), 2816)
    n_da_heads = da_w // HEAD_W
    return (y.reshape(bsz, seq, d), k_rows.reshape(bsz, seq, n_da_heads, HEAD_W),
            v_rows.reshape(bsz, seq, n_da_heads, HEAD_W), s_new, conv_new)


def kernel(x_prompt, x_sample, cache_k, cache_v, page_table, state_hgrn, state_conv, norm_mix_pre, w_in, hg_lb_logits, hg_norm, da_lambda, da_subln, w_branch_a, w_branch_b, w_out, norm_mix_post, norm_ffn_pre, w_ffn_gate, w_ffn_up, conv_w, conv_b, w_ffn_down, norm_ffn_post):
    depth = w_in.shape[0]
    lb_all = jnp.cumsum(jax.nn.softmax(hg_lb_logits.astype(F32), axis=0), axis=0)
    bp, lp = x_prompt.shape[:2]
    ls = x_sample.shape[1]
    past_len = page_table.shape[1] * PAGE_SIZE
    pos_p = jnp.arange(lp)
    pos_s = past_len + jnp.arange(ls)
    n_hg_heads = state_hgrn.shape[2]
    d_ff = state_conv.shape[-1]
    yp, ys = x_prompt, x_sample
    outs = [[] for _ in range(8)]
    for l in range(depth):
        p = dict(norm_mix_pre=norm_mix_pre[l], w_in=w_in[l], hg_norm=hg_norm[l], da_lambda=da_lambda[l],
                 da_subln=da_subln[l], w_branch_a=w_branch_a[l], w_branch_b=w_branch_b[l], w_out=w_out[l],
                 norm_mix_post=norm_mix_post[l], norm_ffn_pre=norm_ffn_pre[l], w_ffn_gate=w_ffn_gate[l],
                 w_ffn_up=w_ffn_up[l], conv_w=conv_w[l], conv_b=conv_b[l], w_ffn_down=w_ffn_down[l],
                 norm_ffn_post=norm_ffn_post[l])

        def attend_prompt(q, k, v, k_rows, v_rows, lam_vec, norm_w, post_scale):
            return _flash_attention(q, k, v, lam_vec, norm_w, post_scale, _tile(q.shape[1], 512), 1)

        def attend_sample(q, k, v, k_rows, v_rows, lam_vec, norm_w, post_scale, l=l):
            width = q.shape[-1]
            rows_per_page = PAGE_SIZE * (width // HEAD_W)
            ck = cache_k[l].reshape(cache_k.shape[1], rows_per_page, HEAD_W)
            cv = cache_v[l].reshape(cache_v.shape[1], rows_per_page, HEAD_W)
            return _paged_attention(q, k_rows.reshape(q.shape), v_rows.reshape(q.shape), ck, cv, page_table,
                                    lam_vec, norm_w, post_scale,
                                    _tile(page_table.shape[1], 8))

        yp, kp, vp, sp, cp = _layer(
            yp, pos_p, attend_prompt, jnp.zeros((bp, n_hg_heads, HEAD_W, HEAD_W), F32),
            jnp.zeros((bp, CONV_W - 1, d_ff), F32), p, lb_all[l], l)
        ys, kss, vss, sss, css = _layer(ys, pos_s, attend_sample, state_hgrn[l], state_conv[l], p, lb_all[l], l)
        for lst, val in zip(outs, (kp, vp, sp, cp, kss, vss, sss, css)):
            lst.append(val)
    kp, vp, sp, cp, kss, vss, sss, css = (jnp.stack(o) for o in outs)
    return (yp, ys, kp, vp, sp.astype(state_hgrn.dtype), cp, kss, vss, sss.astype(state_hgrn.dtype), css)
```

```python
import functools
import math

import jax
import jax.numpy as jnp
from jax import lax
from jax.experimental import pallas as pl
from jax.experimental.pallas import tpu as pltpu

F32 = jnp.float32
BF16 = jnp.bfloat16

RMS_EPS = 1e-6
ROPE_THETA = 500000.0
HEAD_W = 128
DA_HEAD_DIM = 64
ROT_DIM = DA_HEAD_DIM // 4
PAGE_SIZE = 128
CONV_W = 3
SUBLANES = 8
MXU_WIDTH = 256
VMEM_LIMIT = 56 * 1024 * 1024


def _cparams(*sem):
    return pltpu.CompilerParams(dimension_semantics=sem, vmem_limit_bytes=VMEM_LIMIT)


def _sigmoid(x):
    return 1.0 / (1.0 + jnp.exp(-x))


def _rms(x, w):
    return x * lax.rsqrt(jnp.mean(x * x, axis=-1, keepdims=True) + RMS_EPS) * w


def _dot(a, b):
    return jnp.dot(a, b, preferred_element_type=F32)


def _dot_nt(a, b):
    return lax.dot_general(a, b, (((1,), (1,)), ((), ())), preferred_element_type=F32)


def _dot_tn(a, b):
    return lax.dot_general(a, b, (((0,), (0,)), ((), ())), preferred_element_type=F32)


def _norm_kernel(x_ref, w_ref, o_ref):
    o_ref[...] = _rms(x_ref[...], w_ref[...]).astype(o_ref.dtype)


def _norm(x, w, tm):
    t, d = x.shape
    return pl.pallas_call(
        _norm_kernel,
        grid=(t // tm,),
        in_specs=[pl.BlockSpec((tm, d), lambda i: (i, 0)), pl.BlockSpec((1, d), lambda i: (0, 0))],
        out_specs=pl.BlockSpec((tm, d), lambda i: (i, 0)),
        out_shape=jax.ShapeDtypeStruct((t, d), BF16),
        compiler_params=_cparams("parallel"),
        name="pre_norm",
    )(x, w.reshape(1, d))


def _rotary_tile(a, cos, sin_lo, sin_hi):
    return a * cos + pltpu.roll(a, ROT_DIM // 2, 1) * sin_hi + pltpu.roll(a, HEAD_W - ROT_DIM // 2, 1) * sin_lo


def _proj_kernel(*refs, mode, n_tab):
    h_ref, w_ref = refs[0], refs[1]
    tabs = refs[2:2 + n_tab]
    outs = refs[2 + n_tab:-1]
    wbf_ref = refs[-1]

    @pl.when(pl.program_id(1) == 0)
    def _():
        wbf_ref[...] = w_ref[...].astype(BF16)

    acc = _dot(h_ref[...], wbf_ref[...])
    n_grp = acc.shape[1] // HEAD_W
    if mode == "plain":
        outs[0][...] = acc
    elif mode == "sigmoid":
        outs[0][...] = _sigmoid(acc).astype(BF16)
    elif mode == "v":
        for g in range(acc.shape[1] // HEAD_W):
            outs[0][pl.ds(g, acc.shape[0], stride=n_grp), :] = acc[:, g * HEAD_W:(g + 1) * HEAD_W]
        outs[1][...] = acc.astype(BF16)
    else:
        cos, sin_lo, sin_hi = (t[...] for t in tabs)
        for g in range(acc.shape[1] // HEAD_W):
            sl = slice(g * HEAD_W, (g + 1) * HEAD_W)
            r = _rotary_tile(acc[:, sl], cos, sin_lo, sin_hi)
            if mode == "rot_q":
                outs[0][:, sl] = (r * (DA_HEAD_DIM ** -0.5)).astype(BF16)
            else:
                outs[0][pl.ds(g, acc.shape[0], stride=n_grp), :] = r
                outs[1][:, sl] = r.astype(BF16)


def _in_proj(h, w_in, col_off, n_cols, mode, tm, tn, tables=()):
    t, k = h.shape
    out_dtypes = {"plain": (F32,), "sigmoid": (BF16,), "v": (F32, BF16), "rot_q": (BF16,), "rot_k": (F32, BF16)}[mode]
    out_specs = [pl.BlockSpec((tm, tn), lambda j, i: (i, j)) for _ in out_dtypes]
    out_shape = [jax.ShapeDtypeStruct((t, n_cols), dt) for dt in out_dtypes]
    if mode in ("v", "rot_k"):
        assert n_cols == tn
        n_grp = tn // HEAD_W
        out_specs[0] = pl.BlockSpec((tm * n_grp, HEAD_W), lambda j, i: (i, 0))
        out_shape[0] = jax.ShapeDtypeStruct((t * n_grp, HEAD_W), F32)
    blk_off = col_off // tn
    in_specs = [pl.BlockSpec((tm, k), lambda j, i: (i, 0)),
                pl.BlockSpec((k, tn), lambda j, i: (0, j + blk_off))]
    tab_tiles = tables[0].shape[0] // tm if tables else 1
    in_specs += [pl.BlockSpec((tm, HEAD_W), lambda j, i: (i % tab_tiles, 0)) for _ in tables]
    outs = pl.pallas_call(
        functools.partial(_proj_kernel, mode=mode, n_tab=len(tables)),
        grid=(n_cols // tn, t // tm),
        in_specs=in_specs,
        out_specs=out_specs,
        out_shape=out_shape,
        scratch_shapes=[pltpu.VMEM((k, tn), BF16)],
        compiler_params=_cparams("arbitrary", "arbitrary"),
        name="in_proj_" + mode,
    )(h, w_in, *tables)
    return outs


def _rotary_tables(pos):
    half = ROT_DIM // 2
    inv_freq = jnp.power(jnp.float32(ROPE_THETA), -jnp.arange(half, dtype=F32) * (2.0 / ROT_DIM))
    ang = pos.astype(F32)[:, None] * inv_freq[None, :]
    cos, sin = jnp.cos(ang), jnp.sin(ang)
    lane = jnp.arange(HEAD_W) % DA_HEAD_DIM
    idx = lane % half
    is_lo, is_hi = lane < half, (lane >= half) & (lane < ROT_DIM)
    cos_t = jnp.where((is_lo | is_hi)[None, :], cos[:, idx], 1.0)
    sin_lo = jnp.where(is_lo[None, :], -sin[:, idx], 0.0)
    sin_hi = jnp.where(is_hi[None, :], sin[:, idx], 0.0)
    return cos_t, sin_lo, sin_hi


def _hgrn_kernel(hq_ref, hf_ref, hi_ref, hg_ref, lb_ref, nw_ref, s0_ref, o_ref, sout_ref,
                 st_ref, b_ref, k_ref, v_ref, *, n_heads, chunk, n_chunks, l_valid):
    c_len = chunk
    step = pl.program_id(2)

    @pl.when(step == 0)
    def _():
        for hh in range(n_heads):
            st_ref[hh] = s0_ref[hh].T

    row = lax.broadcasted_iota(jnp.int32, (c_len, c_len), 0)
    col = lax.broadcasted_iota(jnp.int32, (c_len, c_len), 1)
    tri = jnp.where(col <= row, 1.0, 0.0).astype(BF16)
    level_masks = {}
    h = SUBLANES
    while h < c_len:
        same_blk = jnp.bitwise_and(jnp.bitwise_xor(row, col), -2 * h) == 0
        level_masks[h] = same_blk & (jnp.bitwise_and(row, h) != 0) & (jnp.bitwise_and(col, h) == 0)
        h *= 2
    sub_t = lax.broadcasted_iota(jnp.int32, (SUBLANES, HEAD_W), 0)

    def head_chunk(hh, r0):
        rows = pl.ds(r0, c_len)
        lanes = slice(hh * HEAD_W, (hh + 1) * HEAD_W)
        lb = lb_ref[:, lanes]
        hq, hf, hi, hg = hq_ref[rows, lanes], hf_ref[rows, lanes], hi_ref[rows, lanes], hg_ref[rows, lanes]
        f = lb + (1.0 - lb) * _sigmoid(hf)
        logf = jnp.log(f)
        kk = 1.0 - f
        if l_valid < c_len * n_chunks:
            valid = (lax.broadcasted_iota(jnp.int32, (c_len, HEAD_W), 0) + r0) < l_valid
            logf = jnp.where(valid, logf, 0.0)
            kk = jnp.where(valid, kk, 0.0)
        q = hq * _sigmoid(hq)
        p_hi = logf.astype(BF16)
        rem = logf - p_hi.astype(F32)
        p_mid = rem.astype(BF16)
        p_lo = (rem - p_mid.astype(F32)).astype(BF16)
        csum = _dot(tri, jnp.concatenate([p_hi, p_mid, p_lo], axis=1))
        b = csum[:, 0:HEAD_W] + csum[:, HEAD_W:2 * HEAD_W] + csum[:, 2 * HEAD_W:3 * HEAD_W]
        b_ref[hh] = b
        k_ref[hh] = kk
        v_ref[hh] = hi
        b_last = b[c_len - 1:c_len, :]

        diag_blocks = []
        for blk in range(c_len // SUBLANES):
            rs = slice(blk * SUBLANES, (blk + 1) * SUBLANES)
            q_blk, b_blk = q[rs, :], b[rs, :]
            od = jnp.zeros((SUBLANES, HEAD_W), F32)
            for s in range(SUBLANES):
                r = blk * SUBLANES + s
                b_s = b_ref[hh, pl.ds(r, 1), :]
                k_s = k_ref[hh, pl.ds(r, 1), :]
                v_s = v_ref[hh, pl.ds(r, 1), :]
                e = jnp.where(sub_t >= s, jnp.exp(b_blk - b_s), 0.0)
                a_col = jnp.sum(q_blk * k_s * e, axis=1, keepdims=True)
                od = od + a_col * v_s
            diag_blocks.append(od)
        o = jnp.concatenate(diag_blocks, axis=0) if len(diag_blocks) > 1 else diag_blocks[0]

        a_off = None
        h = SUBLANES
        while h < c_len:
            grp = 2 * h
            pieces = [jnp.broadcast_to(b_ref[hh, pl.ds(gi * grp + h - 1, 1), :], (grp, HEAD_W))
                      for gi in range(c_len // grp)]
            ref = jnp.concatenate(pieces, axis=0) if len(pieces) > 1 else pieces[0]
            q_h = q * jnp.exp(jnp.minimum(b - ref, 0.0))
            k_h = kk * jnp.exp(jnp.minimum(ref - b, 0.0))
            a_h = _dot_nt(q_h.astype(BF16), k_h.astype(BF16))
            a_h = jnp.where(level_masks[h], a_h, 0.0)
            a_off = a_h if a_off is None else a_off + a_h
            h = grp
        v_bf = hi.astype(BF16)
        if a_off is not None:
            o = o + _dot(a_off.astype(BF16), v_bf)

        st = st_ref[hh]
        o = o + _dot_nt((q * jnp.exp(b)).astype(BF16), st.astype(BF16))
        k_dec = kk * jnp.exp(b_last - b)
        st_ref[hh] = st * jnp.exp(b_last) + _dot_tn(v_bf, k_dec.astype(BF16))

        o_ref[rows, lanes] = (_rms(o, nw_ref[...]) * (hg * _sigmoid(hg))).astype(o_ref.dtype)

    def one_chunk(c, carry):
        r0 = pl.multiple_of(c * c_len, c_len)
        for hh in range(n_heads):
            head_chunk(hh, r0)
        return carry

    lax.fori_loop(0, n_chunks, one_chunk, 0)

    @pl.when(step == pl.num_programs(2) - 1)
    def _():
        for hh in range(n_heads):
            sout_ref[hh] = st_ref[hh].T


def _hgrn(gates, lb, norm_w, s0, l_valid, block_len, chunk, heads_per_step):
    bsz, seq, width = gates.shape
    n_heads = width // (4 * HEAD_W)
    hb = heads_per_step
    n_grp = n_heads // hb

    def gate_spec(g):
        return pl.BlockSpec((None, block_len, hb * HEAD_W), lambda b, h, s: (b, s, g * n_grp + h))

    head_vec = pl.BlockSpec((1, hb * HEAD_W), lambda b, h, s: (0, h))
    state_spec = pl.BlockSpec((None, hb, HEAD_W, HEAD_W), lambda b, h, s: (b, h, 0, 0))
    return pl.pallas_call(
        functools.partial(_hgrn_kernel, n_heads=hb, chunk=chunk, n_chunks=block_len // chunk, l_valid=l_valid),
        grid=(bsz, n_grp, seq // block_len),
        in_specs=[gate_spec(0), gate_spec(1), gate_spec(2), gate_spec(3), head_vec,
                  pl.BlockSpec((1, HEAD_W), lambda b, h, s: (0, 0)), state_spec],
        out_specs=[pl.BlockSpec((None, block_len, hb * HEAD_W), lambda b, h, s: (b, s, h)), state_spec],
        out_shape=[jax.ShapeDtypeStruct((bsz, seq, n_heads * HEAD_W), BF16),
                   jax.ShapeDtypeStruct(s0.shape, F32)],
        scratch_shapes=[pltpu.VMEM((hb, HEAD_W, HEAD_W), F32)] + [pltpu.VMEM((hb, chunk, HEAD_W), F32)] * 3,
        compiler_params=_cparams("parallel", "parallel", "arbitrary"),
        name="hgrn",
    )(gates, gates, gates, gates, lb.reshape(1, -1), norm_w.reshape(1, HEAD_W), s0)


def _flash_kernel(q_ref, k_ref, v_ref, lam_ref, nw_ref, o_ref, qs_ref, sa_ref, sb_ref, m_ref, acc_ref,
                  *, tq, n_heads, post_scale):
    qi = pl.program_id(2)
    n_rows = 2 * tq
    lane = lax.broadcasted_iota(jnp.int32, (tq, HEAD_W), 1)
    zero = jnp.zeros((tq, HEAD_W), BF16)
    for hh in range(n_heads):
        q = q_ref[:, hh * HEAD_W:(hh + 1) * HEAD_W]
        qs_ref[hh, 0:tq, :] = jnp.where(lane < DA_HEAD_DIM, q, zero)
        qs_ref[hh, tq:n_rows, :] = jnp.where(lane >= DA_HEAD_DIM, q, zero)
    m_ref[...] = jnp.full(m_ref.shape, -jnp.inf, F32)
    acc_ref[...] = jnp.zeros(acc_ref.shape, F32)
    lane_reps = tq // HEAD_W
    ones_blk = jnp.ones((tq, HEAD_W), BF16)

    def tile_rows(t):
        return pl.ds(pl.multiple_of(t * tq, tq), tq)

    def scores(t, s_ref):
        for hh in range(n_heads):
            s_ref[hh] = _dot_nt(qs_ref[hh], k_ref[tile_rows(t), hh * HEAD_W:(hh + 1) * HEAD_W])

    def consume(t, s_ref, masked):
        for hh in range(n_heads):
            if masked:
                r = lax.broadcasted_iota(jnp.int32, (n_rows, tq), 0)
                c = lax.broadcasted_iota(jnp.int32, (n_rows, tq), 1)
                s_ref[hh] = jnp.where(c <= jnp.where(r >= tq, r - tq, r), s_ref[hh], -jnp.inf)
            part = s_ref[hh, :, 0:HEAD_W]
            for j in range(1, lane_reps):
                part = jnp.maximum(part, s_ref[hh, :, j * HEAD_W:(j + 1) * HEAD_W])
            m_old = m_ref[hh]
            m_new = jnp.maximum(m_old, jnp.max(part, axis=1, keepdims=True))
            alpha = jnp.exp(m_old - m_new)
            m_ref[hh] = m_new
            p = jnp.exp(s_ref[hh] - jnp.concatenate([m_new] * lane_reps, axis=1)).astype(BF16)
            v = v_ref[tile_rows(t), hh * HEAD_W:(hh + 1) * HEAD_W]
            pv = _dot(p, jnp.concatenate([v, ones_blk], axis=1))
            acc_ref[hh] = jnp.concatenate([alpha, alpha], axis=1) * acc_ref[hh] + pv

    scores(0, sa_ref)

    def pair(j, carry):
        t = 2 * j
        scores(t + 1, sb_ref)
        consume(t, sa_ref, False)
        scores(t + 2, sa_ref)
        consume(t + 1, sb_ref, False)
        return carry

    lax.fori_loop(0, lax.shift_right_logical(qi, 1), pair, 0)
    odd = lax.bitwise_and(qi, 1)

    @pl.when(odd == 1)
    def _():
        scores(qi, sb_ref)
        consume(qi - 1, sa_ref, False)
        consume(qi, sb_ref, True)

    @pl.when(odd == 0)
    def _():
        consume(qi, sa_ref, True)

    for hh in range(n_heads):
        out = acc_ref[hh, :, 0:HEAD_W] / acc_ref[hh, :, HEAD_W:2 * HEAD_W]
        ob = out[0:tq, :] - lam_ref[...] * out[tq:n_rows, :]
        o_ref[:, hh * HEAD_W:(hh + 1) * HEAD_W] = (_rms(ob, nw_ref[...]) * post_scale).astype(o_ref.dtype)


def _flash_attention(q, k, v, lam_vec, norm_w, post_scale, tq, heads_per_step):
    bsz, seq, width = q.shape
    hb = heads_per_step
    kv_spec = pl.BlockSpec((None, seq, hb * HEAD_W), lambda b, h, i: (b, 0, h))
    q_spec = pl.BlockSpec((None, tq, hb * HEAD_W), lambda b, h, i: (b, i, h))
    vec = pl.BlockSpec((1, HEAD_W), lambda b, h, i: (0, 0))
    return pl.pallas_call(
        functools.partial(_flash_kernel, tq=tq, n_heads=hb, post_scale=post_scale),
        grid=(bsz, width // (hb * HEAD_W), seq // tq),
        in_specs=[q_spec, kv_spec, kv_spec, vec, vec],
        out_specs=q_spec,
        out_shape=jax.ShapeDtypeStruct((bsz, seq, width), BF16),
        scratch_shapes=[pltpu.VMEM((hb, 2 * tq, HEAD_W), BF16), pltpu.VMEM((hb, 2 * tq, tq), F32),
                        pltpu.VMEM((hb, 2 * tq, tq), F32), pltpu.VMEM((hb, 2 * tq, HEAD_W), F32),
                        pltpu.VMEM((hb, 2 * tq, 2 * HEAD_W), F32)],
        compiler_params=_cparams("parallel", "parallel", "arbitrary"),
        name="flash_attention",
    )(q, k, v, lam_vec, norm_w.reshape(1, HEAD_W))


def _lane_to_col(vec):
    n = vec.shape[1]
    r = lax.broadcasted_iota(jnp.int32, (n, n), 0)
    c = lax.broadcasted_iota(jnp.int32, (n, n), 1)
    return jnp.sum(jnp.where(r == c, jnp.broadcast_to(vec, (n, n)), 0.0), axis=1, keepdims=True)


def _paged_kernel(pt_ref, q_ref, k_hbm, v_hbm, knew_ref, vnew_ref, lam_ref, nw_ref, o_ref,
                  wt_ref, m_ref, l_ref, acc_ref, kbuf, vbuf, sem, *, n_pp, n_heads, l_new, n_slots, post_scale):
    n_g = pl.num_programs(1)
    g = pl.program_id(1)
    step = pl.program_id(0) * n_g + g
    n_steps = pl.num_programs(0) * n_g

    def page_copies(s, slot):
        sb = lax.div(s, n_g)
        first = (s - sb * n_g) * n_pp
        copies = []
        for i in range(n_pp):
            page = pt_ref[sb, first + i]
            copies.append(pltpu.make_async_copy(k_hbm.at[page], kbuf.at[slot, i], sem.at[slot, 0]))
            copies.append(pltpu.make_async_copy(v_hbm.at[page], vbuf.at[slot, i], sem.at[slot, 1]))
        return copies

    @pl.when(step == 0)
    def _():
        for s in range(n_slots - 1):
            for cp in page_copies(s, s):
                cp.start()

    ahead = step + (n_slots - 1)

    @pl.when(ahead < n_steps)
    def _():
        for cp in page_copies(ahead, lax.rem(ahead, n_slots)):
            cp.start()

    @pl.when(g == 0)
    def _():
        m_ref[...] = jnp.full(m_ref.shape, -jnp.inf, F32)
        l_ref[...] = jnp.zeros(l_ref.shape, F32)
        acc_ref[...] = jnp.zeros(acc_ref.shape, F32)
        q_rows = jnp.concatenate([q_ref[...].astype(F32)] * (SUBLANES // l_new), axis=0)
        q_rep = jnp.concatenate([q_rows] * (HEAD_W // SUBLANES), axis=0)
        col = lax.broadcasted_iota(jnp.int32, q_rep.shape, 0)
        lane = lax.broadcasted_iota(jnp.int32, q_rep.shape, 1)
        own = (lax.div(lane, DA_HEAD_DIM) == lax.div(col, l_new)) & (col < n_heads * 2 * l_new)
        wt_ref[...] = jnp.where(own, q_rep, 0.0).astype(BF16)

    slot = lax.rem(step, n_slots)
    for cp in page_copies(step, slot):
        cp.wait()
    w_t = wt_ref[...]

    def softmax_step(s_t):
        m_old = m_ref[...]
        m_new = jnp.maximum(m_old, jnp.max(s_t, axis=0, keepdims=True))
        alpha = jnp.exp(m_old - m_new)
        p = jnp.exp(s_t - m_new)
        l_ref[...] = alpha * l_ref[...] + jnp.sum(p, axis=0, keepdims=True)
        m_ref[...] = m_new
        return _lane_to_col(alpha), p

    def page_rows(ref):
        heads = [ref[pl.ds(h, PAGE_SIZE, stride=n_heads), :].astype(BF16) for h in range(n_heads)]
        return jnp.concatenate(heads, axis=1)

    s_pages = [_dot_nt(page_rows(kbuf.at[slot, i]), w_t) for i in range(n_pp)]
    v_pages = [page_rows(vbuf.at[slot, i]) for i in range(n_pp)]
    alpha_col, p = softmax_step(jnp.concatenate(s_pages, axis=0))
    acc_ref[...] = alpha_col * acc_ref[...] + _dot(p.T.astype(BF16), jnp.concatenate(v_pages, axis=0))

    @pl.when(g == pl.num_programs(1) - 1)
    def _():
        s_new = _dot_nt(knew_ref[...].astype(BF16), w_t)
        s_row = lax.broadcasted_iota(jnp.int32, s_new.shape, 0)
        t_col = lax.rem(lax.broadcasted_iota(jnp.int32, s_new.shape, 1), l_new)
        alpha_col, p = softmax_step(jnp.where(s_row <= t_col, s_new, -jnp.inf))
        acc = alpha_col * acc_ref[...]
        for s in range(l_new):
            acc = acc + _lane_to_col(p[s:s + 1, :]) * vnew_ref[s:s + 1, :]
        acc_ref[...] = acc
        inv_l = 1.0 / _lane_to_col(l_ref[...])
        lam = lam_ref[...]
        for h in range(n_heads):
            rs = slice(h * 2 * l_new, (h + 1) * 2 * l_new)
            cs = slice(h * HEAD_W, (h + 1) * HEAD_W)
            maps = acc_ref[rs, cs] * inv_l[rs, :]
            ob = maps - lam * pltpu.roll(maps, l_new, 0)
            o_ref[:, cs] = _rms(ob, nw_ref[...]) * post_scale


def _paged_attention(q, k_new, v_new, cache_k, cache_v, page_table, lam_vec, norm_w, post_scale, n_pp):
    bsz, l_new, width = q.shape
    n_heads = width // HEAD_W
    n_pages = page_table.shape[1]
    assert 2 * l_new == SUBLANES and n_heads * 2 * l_new <= HEAD_W and n_pages % n_pp == 0
    pad_rows = ((0, 0), (0, SUBLANES - l_new), (0, 0))
    k_new, v_new = jnp.pad(k_new, pad_rows), jnp.pad(v_new, pad_rows)
    n_slots = 3
    page_shape = (PAGE_SIZE * n_heads, HEAD_W)
    new_spec = pl.BlockSpec((None, SUBLANES, width), lambda b, g, pt: (b, 0, 0))
    vec = pl.BlockSpec((1, HEAD_W), lambda b, g, pt: (0, 0))
    hbm = pl.BlockSpec(memory_space=pl.ANY)
    out = pl.pallas_call(
        functools.partial(_paged_kernel, n_pp=n_pp, n_heads=n_heads, l_new=l_new, n_slots=n_slots,
                          post_scale=post_scale),
        grid_spec=pltpu.PrefetchScalarGridSpec(
            num_scalar_prefetch=1,
            grid=(bsz, n_pages // n_pp),
            in_specs=[pl.BlockSpec((None, l_new, width), lambda b, g, pt: (b, 0, 0)), hbm, hbm,
                      new_spec, new_spec, vec, vec],
            out_specs=new_spec,
            scratch_shapes=[pltpu.VMEM((HEAD_W, width), BF16), pltpu.VMEM((1, HEAD_W), F32),
                            pltpu.VMEM((1, HEAD_W), F32), pltpu.VMEM((HEAD_W, width), F32),
                            pltpu.VMEM((n_slots, n_pp) + page_shape, F32),
                            pltpu.VMEM((n_slots, n_pp) + page_shape, F32),
                            pltpu.SemaphoreType.DMA((n_slots, 2))],
        ),
        out_shape=jax.ShapeDtypeStruct((bsz, SUBLANES, width), F32),
        compiler_params=_cparams("arbitrary", "arbitrary"),
        name="paged_attention",
    )(page_table, q, cache_k, cache_v, k_new, v_new, lam_vec, norm_w.reshape(1, HEAD_W))
    return out[:, :l_new, :].astype(BF16)


def _merge_kernel(oa_ref, ob_ref, wa_ref, wb_ref, ga_ref, gb_ref, u_ref, wa_bf, wb_bf):
    @pl.when(pl.program_id(1) == 0)
    def _():
        wa_bf[...] = wa_ref[...].astype(BF16)
        wb_bf[...] = wb_ref[...].astype(BF16)

    u = ga_ref[...] * _dot(oa_ref[...], wa_bf[...]) + gb_ref[...] * _dot(ob_ref[...], wb_bf[...])
    u_ref[...] = u.astype(u_ref.dtype)


def _merge(oa, ob, w_a, w_b, sig_gates, tm, tn):
    t, k = oa.shape
    n = w_a.shape[1]
    nj = n // tn
    row = pl.BlockSpec((tm, k), lambda j, i: (i, 0))
    wsp = pl.BlockSpec((k, tn), lambda j, i: (0, j))
    return pl.pallas_call(
        _merge_kernel,
        grid=(nj, t // tm),
        in_specs=[row, row, wsp, wsp, pl.BlockSpec((tm, tn), lambda j, i: (i, j)),
                  pl.BlockSpec((tm, tn), lambda j, i: (i, j + nj))],
        out_specs=pl.BlockSpec((tm, tn), lambda j, i: (i, j)),
        out_shape=jax.ShapeDtypeStruct((t, n), BF16),
        scratch_shapes=[pltpu.VMEM((k, tn), BF16), pltpu.VMEM((k, tn), BF16)],
        compiler_params=_cparams("arbitrary", "arbitrary"),
        name="branch_merge",
    )(oa, ob, w_a, w_b, sig_gates, sig_gates)


def _outproj_kernel(u_ref, w_ref, x_ref, nw_post_ref, nw_pre_ref, x1_ref, h2_ref):
    y = _dot(u_ref[...], w_ref[...])
    x1 = x_ref[...] + _rms(y, nw_post_ref[...])
    x1_ref[...] = x1
    h2_ref[...] = _rms(x1, nw_pre_ref[...]).astype(h2_ref.dtype)


def _outproj(u, w_out_bf, x, nw_post, nw_pre, tm):
    t, d = x.shape
    row = pl.BlockSpec((tm, d), lambda i: (i, 0))
    vec = pl.BlockSpec((1, d), lambda i: (0, 0))
    return pl.pallas_call(
        _outproj_kernel,
        grid=(t // tm,),
        in_specs=[row, pl.BlockSpec((d, d), lambda i: (0, 0)), row, vec, vec],
        out_specs=[row, row],
        out_shape=[jax.ShapeDtypeStruct((t, d), F32), jax.ShapeDtypeStruct((t, d), BF16)],
        compiler_params=_cparams("parallel"),
        name="out_proj",
    )(u, w_out_bf, x, nw_post.reshape(1, d), nw_pre.reshape(1, d))


def _gelu_tanh(x):
    return 0.5 * x * (1.0 + jnp.tanh(math.sqrt(2.0 / math.pi) * (x + 0.044715 * (x * x * x))))


def _ffn_up_kernel(*refs, seq_len, tiles_per_seq):
    if tiles_per_seq:
        h_ref, wg_ref, wu_ref, cw_ref, cb_ref, prev_ref, ff_ref, tail_ref, wg_bf, wu_bf, carry_ref = refs
    else:
        h_ref, wg_ref, wu_ref, cw_ref, cb_ref, halo1_ref, halo2_ref, ff_ref, g_ref, wg_bf, wu_bf = refs
    i = pl.program_id(1)

    @pl.when(i == 0)
    def _():
        wg_bf[...] = wg_ref[...].astype(BF16)
        wu_bf[...] = wu_ref[...].astype(BF16)

    if tiles_per_seq:
        @pl.when(i % tiles_per_seq == 0)
        def _():
            carry_ref[...] = prev_ref[...]

    h = h_ref[...]
    tm, tn = ff_ref.shape
    cn = min(tn, MXU_WIDTH)
    row = lax.broadcasted_iota(jnp.int32, (tm, cn), 0)
    for c in range(tn // cn):
        cs = slice(c * cn, (c + 1) * cn)
        g = _dot(h, wg_bf[:, cs])
        up = _dot(h, wu_bf[:, cs])
        g1 = pltpu.roll(g, 1, 0)
        g2 = pltpu.roll(g, 2, 0)
        if tiles_per_seq:
            p0, p1 = carry_ref[0:1, cs], carry_ref[1:2, cs]
            g1 = jnp.where(row == 0, p1, g1)
            g2 = jnp.where(row == 0, p0, jnp.where(row == 1, p1, g2))
            carry_ref[:, cs] = g[tm - 2:tm, :]
            tail_ref[:, cs] = g[tm - 2:tm, :]
        else:
            t_in_seq = lax.rem(row, seq_len)
            g1 = jnp.where(t_in_seq >= 1, g1, halo1_ref[:, cs])
            g2 = jnp.where(t_in_seq >= 2, g2, halo2_ref[:, cs])
            g_ref[:, cs] = g
        conv = cb_ref[:, cs] + g2 * cw_ref[0:1, cs] + g1 * cw_ref[1:2, cs] + g * cw_ref[2:3, cs]
        ff_ref[:, cs] = (_gelu_tanh(conv) * up).astype(ff_ref.dtype)


def _ffn_up(h2, w_gate, w_up, conv_w, conv_b, conv_prev, seq_len, tm, tn):
    t, d = h2.shape
    f = w_gate.shape[1]
    bsz = t // seq_len
    row = pl.BlockSpec((tm, d), lambda j, i: (i, 0))
    wsp = pl.BlockSpec((d, tn), lambda j, i: (0, j))
    cw = pl.BlockSpec((CONV_W, tn), lambda j, i: (0, j))
    cb = pl.BlockSpec((1, tn), lambda j, i: (0, j))
    tile = pl.BlockSpec((tm, tn), lambda j, i: (i, j))
    scratch = [pltpu.VMEM((d, tn), BF16), pltpu.VMEM((d, tn), BF16)]
    if seq_len % tm == 0:
        tps = seq_len // tm
        state = pl.BlockSpec((None, CONV_W - 1, tn), lambda j, i: (i // tps, 0, j))
        ff, tail = pl.pallas_call(
            functools.partial(_ffn_up_kernel, seq_len=seq_len, tiles_per_seq=tps),
            grid=(f // tn, t // tm),
            in_specs=[row, wsp, wsp, cw, cb, state],
            out_specs=[tile, state],
            out_shape=[jax.ShapeDtypeStruct((t, f), BF16), jax.ShapeDtypeStruct((bsz, CONV_W - 1, f), F32)],
            scratch_shapes=scratch + [pltpu.VMEM((CONV_W - 1, tn), F32)],
            compiler_params=_cparams("arbitrary", "arbitrary"),
            name="ffn_up_long",
        )(h2, w_gate, w_up, conv_w, conv_b.reshape(1, f), conv_prev)
        return ff, tail
    assert tm % seq_len == 0 and seq_len >= CONV_W - 1
    zeros = jnp.zeros((bsz, seq_len - 1, f), F32)
    halo1 = jnp.concatenate([conv_prev[:, 1:2], zeros], axis=1).reshape(t, f)
    halo2 = jnp.concatenate([conv_prev, zeros[:, 1:]], axis=1).reshape(t, f)
    ff, g = pl.pallas_call(
        functools.partial(_ffn_up_kernel, seq_len=seq_len, tiles_per_seq=0),
        grid=(f // tn, t // tm),
        in_specs=[row, wsp, wsp, cw, cb, tile, tile],
        out_specs=[tile, tile],
        out_shape=[jax.ShapeDtypeStruct((t, f), BF16), jax.ShapeDtypeStruct((t, f), F32)],
        scratch_shapes=scratch,
        compiler_params=_cparams("arbitrary", "arbitrary"),
        name="ffn_up_short",
    )(h2, w_gate, w_up, conv_w, conv_b.reshape(1, f), halo1, halo2)
    return ff, g.reshape(bsz, seq_len, f)[:, seq_len - (CONV_W - 1):, :]


def _ffn_down_kernel(ff_ref, w_ref, x_ref, nw_ref, y_ref, acc_ref):
    k = pl.program_id(1)

    @pl.when(k == 0)
    def _():
        acc_ref[...] = jnp.zeros(acc_ref.shape, F32)

    acc_ref[...] += _dot(ff_ref[...], w_ref[...])

    @pl.when(k == pl.num_programs(1) - 1)
    def _():
        y_ref[...] = x_ref[...] + _rms(acc_ref[...], nw_ref[...])


def _ffn_down(ff, w_down_bf, x1, nw, tm, tk):
    t, f = ff.shape
    d = x1.shape[1]
    row = pl.BlockSpec((tm, d), lambda i, k: (i, 0))
    return pl.pallas_call(
        _ffn_down_kernel,
        grid=(t // tm, f // tk),
        in_specs=[pl.BlockSpec((tm, tk), lambda i, k: (i, k)), pl.BlockSpec((tk, d), lambda i, k: (k, 0)),
                  row, pl.BlockSpec((1, d), lambda i, k: (0, 0))],
        out_specs=row,
        out_shape=jax.ShapeDtypeStruct((t, d), F32),
        scratch_shapes=[pltpu.VMEM((tm, d), F32)],
        compiler_params=_cparams("parallel", "arbitrary"),
        name="ffn_down",
    )(ff, w_down_bf, x1, nw.reshape(1, d))


def _tile(n, pref):
    return pref if n % pref == 0 else n


def _layer(x, pos, attend, hg_s0, conv_prev, p, lb, layer_idx):
    bsz, seq, d = x.shape
    t = bsz * seq
    hg_w = p["w_branch_a"].shape[0]
    da_w = p["w_branch_b"].shape[0]
    x2 = x.reshape(t, d)
    tm = _tile(t, 1024)

    h = _norm(x2, p["norm_mix_pre"], _tile(t, 512))
    w_in = p["w_in"]
    gates = _in_proj(h, w_in, 0, 4 * hg_w, "plain", tm, 1024)[0]
    tabs = _rotary_tables(pos)
    if seq % tm:
        tabs = tuple(jnp.tile(tb, (tm // seq, 1)) for tb in tabs)
    q_bf = _in_proj(h, w_in, 4 * hg_w, da_w, "rot_q", tm, 1024, tabs)[0]
    k_rows, k_bf = _in_proj(h, w_in, 4 * hg_w + da_w, da_w, "rot_k", tm, 1024, tabs)
    v_rows, v_bf = _in_proj(h, w_in, 4 * hg_w + 2 * da_w, da_w, "v", tm, 1024)
    sig_gates = _in_proj(h, w_in, 4 * hg_w + 3 * da_w, 2 * d, "sigmoid", tm, 1024)[0]

    gates3 = gates.reshape(bsz, seq, 4 * hg_w)
    if seq % 64 == 0:
        oa, s_new = _hgrn(gates3, lb, p["hg_norm"], hg_s0, seq, _tile(seq, 512), 256, 8)
    else:
        pad = (-seq) % SUBLANES
        gates3 = jnp.pad(gates3, ((0, 0), (0, pad), (0, 0)))
        oa, s_new = _hgrn(gates3, lb, p["hg_norm"], hg_s0, seq, seq + pad, seq + pad, hg_w // HEAD_W)
        oa = oa[:, :seq, :]
    oa = oa.reshape(t, hg_w)

    lam_init = 0.8 - 0.6 * math.exp(-0.3 * layer_idx)
    lam = p["da_lambda"].astype(F32)
    lam_full = jnp.exp(jnp.sum(lam[0] * lam[1])) - jnp.exp(jnp.sum(lam[2] * lam[3])) + lam_init
    lam_vec = jnp.full((1, HEAD_W), lam_full, F32)
    shp = (bsz, seq, da_w)
    ob = attend(q_bf.reshape(shp), k_bf.reshape(shp), v_bf.reshape(shp), k_rows, v_rows,
                lam_vec, p["da_subln"], 1.0 - lam_init).reshape(t, da_w)

    u = _merge(oa, ob, p["w_branch_a"], p["w_branch_b"], sig_gates, tm, 512)
    x1, h2 = _outproj(u, p["w_out"].astype(BF16), x2, p["norm_mix_post"], p["norm_ffn_pre"], _tile(t, 512))
    ff, conv_new = _ffn_up(h2, p["w_ffn_gate"], p["w_ffn_up"], p["conv_w"], p["conv_b"], conv_prev, seq, tm, 512)
    y = _ffn_down(ff, p["w_ffn_down"].astype(BF16), x1, p["norm_ffn_post"], _tile(t, 512), 2816)
    n_da_heads = da_w // HEAD_W
    return (y.reshape(bsz, seq, d), k_rows.reshape(bsz, seq, n_da_heads, HEAD_W),
            v_rows.reshape(bsz, seq, n_da_heads, HEAD_W), s_new, conv_new)


def kernel(x_prompt, x_sample, cache_k, cache_v, page_table, state_hgrn, state_conv, norm_mix_pre, w_in, hg_lb_logits, hg_norm, da_lambda, da_subln, w_branch_a, w_branch_b, w_out, norm_mix_post, norm_ffn_pre, w_ffn_gate, w_ffn_up, conv_w, conv_b, w_ffn_down, norm_ffn_post):
    depth = w_in.shape[0]
    lb_all = jnp.cumsum(jax.nn.softmax(hg_lb_logits.astype(F32), axis=0), axis=0)
    bp, lp = x_prompt.shape[:2]
    ls = x_sample.shape[1]
    past_len = page_table.shape[1] * PAGE_SIZE
    pos_p = jnp.arange(lp)
    pos_s = past_len + jnp.arange(ls)
    n_hg_heads = state_hgrn.shape[2]
    d_ff = state_conv.shape[-1]
    yp, ys = x_prompt, x_sample
    outs = [[] for _ in range(8)]
    for l in range(depth):
        p = dict(norm_mix_pre=norm_mix_pre[l], w_in=w_in[l], hg_norm=hg_norm[l], da_lambda=da_lambda[l],
                 da_subln=da_subln[l], w_branch_a=w_branch_a[l], w_branch_b=w_branch_b[l], w_out=w_out[l],
                 norm_mix_post=norm_mix_post[l], norm_ffn_pre=norm_ffn_pre[l], w_ffn_gate=w_ffn_gate[l],
                 w_ffn_up=w_ffn_up[l], conv_w=conv_w[l], conv_b=conv_b[l], w_ffn_down=w_ffn_down[l],
                 norm_ffn_post=norm_ffn_post[l])

        def attend_prompt(q, k, v, k_rows, v_rows, lam_vec, norm_w, post_scale):
            return _flash_attention(q, k, v, lam_vec, norm_w, post_scale, _tile(q.shape[1], 512), 1)

        def attend_sample(q, k, v, k_rows, v_rows, lam_vec, norm_w, post_scale, l=l):
            width = q.shape[-1]
            rows_per_page = PAGE_SIZE * (width // HEAD_W)
            ck = cache_k[l].reshape(cache_k.shape[1], rows_per_page, HEAD_W)
            cv = cache_v[l].reshape(cache_v.shape[1], rows_per_page, HEAD_W)
            return _paged_attention(q, k_rows.reshape(q.shape), v_rows.reshape(q.shape), ck, cv, page_table,
                                    lam_vec, norm_w, post_scale,
                                    _tile(page_table.shape[1], 8))

        yp, kp, vp, sp, cp = _layer(
            yp, pos_p, attend_prompt, jnp.zeros((bp, n_hg_heads, HEAD_W, HEAD_W), F32),
            jnp.zeros((bp, CONV_W - 1, d_ff), F32), p, lb_all[l], l)
        ys, kss, vss, sss, css = _layer(ys, pos_s, attend_sample, state_hgrn[l], state_conv[l], p, lb_all[l], l)
        for lst, val in zip(outs, (kp, vp, sp, cp, kss, vss, sss, css)):
            lst.append(val)
    kp, vp, sp, cp, kss, vss, sss, css = (jnp.stack(o) for o in outs)
    return (yp, ys, kp, vp, sp.astype(state_hgrn.dtype), cp, kss, vss, sss.astype(state_hgrn.dtype), css)
```

```python
import functools
import math

import jax
import jax.numpy as jnp
from jax import lax
from jax.experimental import pallas as pl
from jax.experimental.pallas import tpu as pltpu

F32 = jnp.float32
BF16 = jnp.bfloat16

RMS_EPS = 1e-6
ROPE_THETA = 500000.0
HEAD_W = 128
DA_HEAD_DIM = 64
ROT_DIM = DA_HEAD_DIM // 4
PAGE_SIZE = 128
CONV_W = 3
SUBLANES = 8
MXU_WIDTH = 256
VMEM_LIMIT = 56 * 1024 * 1024


def _cparams(*sem):
    return pltpu.CompilerParams(dimension_semantics=sem, vmem_limit_bytes=VMEM_LIMIT)


def _sigmoid(x):
    return 1.0 / (1.0 + jnp.exp(-x))


def _rms(x, w):
    return x * lax.rsqrt(jnp.mean(x * x, axis=-1, keepdims=True) + RMS_EPS) * w


def _dot(a, b):
    return jnp.dot(a, b, preferred_element_type=F32)


def _dot_nt(a, b):
    return lax.dot_general(a, b, (((1,), (1,)), ((), ())), preferred_element_type=F32)


def _dot_tn(a, b):
    return lax.dot_general(a, b, (((0,), (0,)), ((), ())), preferred_element_type=F32)


def _norm_kernel(x_ref, w_ref, o_ref):
    o_ref[...] = _rms(x_ref[...], w_ref[...]).astype(o_ref.dtype)


def _norm(x, w, tm):
    t, d = x.shape
    return pl.pallas_call(
        _norm_kernel,
        grid=(t // tm,),
        in_specs=[pl.BlockSpec((tm, d), lambda i: (i, 0)), pl.BlockSpec((1, d), lambda i: (0, 0))],
        out_specs=pl.BlockSpec((tm, d), lambda i: (i, 0)),
        out_shape=jax.ShapeDtypeStruct((t, d), BF16),
        compiler_params=_cparams("parallel"),
        name="pre_norm",
    )(x, w.reshape(1, d))


def _rotary_tile(a, cos, sin_lo, sin_hi):
    return a * cos + pltpu.roll(a, ROT_DIM // 2, 1) * sin_hi + pltpu.roll(a, HEAD_W - ROT_DIM // 2, 1) * sin_lo


def _proj_kernel(*refs, mode, n_tab):
    h_ref, w_ref = refs[0], refs[1]
    tabs = refs[2:2 + n_tab]
    outs = refs[2 + n_tab:-1]
    wbf_ref = refs[-1]

    @pl.when(pl.program_id(1) == 0)
    def _():
        wbf_ref[...] = w_ref[...].astype(BF16)

    acc = _dot(h_ref[...], wbf_ref[...])
    n_grp = acc.shape[1] // HEAD_W
    if mode == "plain":
        outs[0][...] = acc
    elif mode == "sigmoid":
        outs[0][...] = _sigmoid(acc).astype(BF16)
    elif mode == "v":
        for g in range(acc.shape[1] // HEAD_W):
            outs[0][pl.ds(g, acc.shape[0], stride=n_grp), :] = acc[:, g * HEAD_W:(g + 1) * HEAD_W]
        outs[1][...] = acc.astype(BF16)
    else:
        cos, sin_lo, sin_hi = (t[...] for t in tabs)
        for g in range(acc.shape[1] // HEAD_W):
            sl = slice(g * HEAD_W, (g + 1) * HEAD_W)
            r = _rotary_tile(acc[:, sl], cos, sin_lo, sin_hi)
            if mode == "rot_q":
                outs[0][:, sl] = (r * (DA_HEAD_DIM ** -0.5)).astype(BF16)
            else:
                outs[0][pl.ds(g, acc.shape[0], stride=n_grp), :] = r
                outs[1][:, sl] = r.astype(BF16)


def _in_proj(h, w_in, col_off, n_cols, mode, tm, tn, tables=()):
    t, k = h.shape
    out_dtypes = {"plain": (F32,), "sigmoid": (BF16,), "v": (F32, BF16), "rot_q": (BF16,), "rot_k": (F32, BF16)}[mode]
    out_specs = [pl.BlockSpec((tm, tn), lambda j, i: (i, j)) for _ in out_dtypes]
    out_shape = [jax.ShapeDtypeStruct((t, n_cols), dt) for dt in out_dtypes]
    if mode in ("v", "rot_k"):
        assert n_cols == tn
        n_grp = tn // HEAD_W
        out_specs[0] = pl.BlockSpec((tm * n_grp, HEAD_W), lambda j, i: (i, 0))
        out_shape[0] = jax.ShapeDtypeStruct((t * n_grp, HEAD_W), F32)
    blk_off = col_off // tn
    in_specs = [pl.BlockSpec((tm, k), lambda j, i: (i, 0)),
                pl.BlockSpec((k, tn), lambda j, i: (0, j + blk_off))]
    tab_tiles = tables[0].shape[0] // tm if tables else 1
    in_specs += [pl.BlockSpec((tm, HEAD_W), lambda j, i: (i % tab_tiles, 0)) for _ in tables]
    outs = pl.pallas_call(
        functools.partial(_proj_kernel, mode=mode, n_tab=len(tables)),
        grid=(n_cols // tn, t // tm),
        in_specs=in_specs,
        out_specs=out_specs,
        out_shape=out_shape,
        scratch_shapes=[pltpu.VMEM((k, tn), BF16)],
        compiler_params=_cparams("arbitrary", "arbitrary"),
        name="in_proj_" + mode,
    )(h, w_in, *tables)
    return outs


def _rotary_tables(pos):
    half = ROT_DIM // 2
    inv_freq = jnp.power(jnp.float32(ROPE_THETA), -jnp.arange(half, dtype=F32) * (2.0 / ROT_DIM))
    ang = pos.astype(F32)[:, None] * inv_freq[None, :]
    cos, sin = jnp.cos(ang), jnp.sin(ang)
    lane = jnp.arange(HEAD_W) % DA_HEAD_DIM
    idx = lane % half
    is_lo, is_hi = lane < half, (lane >= half) & (lane < ROT_DIM)
    cos_t = jnp.where((is_lo | is_hi)[None, :], cos[:, idx], 1.0)
    sin_lo = jnp.where(is_lo[None, :], -sin[:, idx], 0.0)
    sin_hi = jnp.where(is_hi[None, :], sin[:, idx], 0.0)
    return cos_t, sin_lo, sin_hi


def _hgrn_kernel(hq_ref, hf_ref, hi_ref, hg_ref, lb_ref, nw_ref, s0_ref, o_ref, sout_ref,
                 st_ref, b_ref, k_ref, v_ref, *, n_heads, chunk, n_chunks, l_valid):
    c_len = chunk
    step = pl.program_id(2)

    @pl.when(step == 0)
    def _():
        for hh in range(n_heads):
            st_ref[hh] = s0_ref[hh].T

    row = lax.broadcasted_iota(jnp.int32, (c_len, c_len), 0)
    col = lax.broadcasted_iota(jnp.int32, (c_len, c_len), 1)
    tri = jnp.where(col <= row, 1.0, 0.0).astype(BF16)
    level_masks = {}
    h = SUBLANES
    while h < c_len:
        same_blk = jnp.bitwise_and(jnp.bitwise_xor(row, col), -2 * h) == 0
        level_masks[h] = same_blk & (jnp.bitwise_and(row, h) != 0) & (jnp.bitwise_and(col, h) == 0)
        h *= 2
    sub_t = lax.broadcasted_iota(jnp.int32, (SUBLANES, HEAD_W), 0)

    def head_chunk(hh, r0):
        rows = pl.ds(r0, c_len)
        lanes = slice(hh * HEAD_W, (hh + 1) * HEAD_W)
        lb = lb_ref[:, lanes]
        hq, hf, hi, hg = hq_ref[rows, lanes], hf_ref[rows, lanes], hi_ref[rows, lanes], hg_ref[rows, lanes]
        f = lb + (1.0 - lb) * _sigmoid(hf)
        logf = jnp.log(f)
        kk = 1.0 - f
        if l_valid < c_len * n_chunks:
            valid = (lax.broadcasted_iota(jnp.int32, (c_len, HEAD_W), 0) + r0) < l_valid
            logf = jnp.where(valid, logf, 0.0)
            kk = jnp.where(valid, kk, 0.0)
        q = hq * _sigmoid(hq)
        p_hi = logf.astype(BF16)
        rem = logf - p_hi.astype(F32)
        p_mid = rem.astype(BF16)
        p_lo = (rem - p_mid.astype(F32)).astype(BF16)
        csum = _dot(tri, jnp.concatenate([p_hi, p_mid, p_lo], axis=1))
        b = csum[:, 0:HEAD_W] + csum[:, HEAD_W:2 * HEAD_W] + csum[:, 2 * HEAD_W:3 * HEAD_W]
        b_ref[hh] = b
        k_ref[hh] = kk
        v_ref[hh] = hi
        b_last = b[c_len - 1:c_len, :]

        diag_blocks = []
        for blk in range(c_len // SUBLANES):
            rs = slice(blk * SUBLANES, (blk + 1) * SUBLANES)
            q_blk, b_blk = q[rs, :], b[rs, :]
            od = jnp.zeros((SUBLANES, HEAD_W), F32)
            for s in range(SUBLANES):
                r = blk * SUBLANES + s
                b_s = b_ref[hh, pl.ds(r, 1), :]
                k_s = k_ref[hh, pl.ds(r, 1), :]
                v_s = v_ref[hh, pl.ds(r, 1), :]
                e = jnp.where(sub_t >= s, jnp.exp(b_blk - b_s), 0.0)
                a_col = jnp.sum(q_blk * k_s * e, axis=1, keepdims=True)
                od = od + a_col * v_s
            diag_blocks.append(od)
        o = jnp.concatenate(diag_blocks, axis=0) if len(diag_blocks) > 1 else diag_blocks[0]

        a_off = None
        h = SUBLANES
        while h < c_len:
            grp = 2 * h
            pieces = [jnp.broadcast_to(b_ref[hh, pl.ds(gi * grp + h - 1, 1), :], (grp, HEAD_W))
                      for gi in range(c_len // grp)]
            ref = jnp.concatenate(pieces, axis=0) if len(pieces) > 1 else pieces[0]
            q_h = q * jnp.exp(jnp.minimum(b - ref, 0.0))
            k_h = kk * jnp.exp(jnp.minimum(ref - b, 0.0))
            a_h = _dot_nt(q_h.astype(BF16), k_h.astype(BF16))
            a_h = jnp.where(level_masks[h], a_h, 0.0)
            a_off = a_h if a_off is None else a_off + a_h
            h = grp
        v_bf = hi.astype(BF16)
        if a_off is not None:
            o = o + _dot(a_off.astype(BF16), v_bf)

        st = st_ref[hh]
        o = o + _dot_nt((q * jnp.exp(b)).astype(BF16), st.astype(BF16))
        k_dec = kk * jnp.exp(b_last - b)
        st_ref[hh] = st * jnp.exp(b_last) + _dot_tn(v_bf, k_dec.astype(BF16))

        o_ref[rows, lanes] = (_rms(o, nw_ref[...]) * (hg * _sigmoid(hg))).astype(o_ref.dtype)

    def one_chunk(c, carry):
        r0 = pl.multiple_of(c * c_len, c_len)
        for hh in range(n_heads):
            head_chunk(hh, r0)
        return carry

    lax.fori_loop(0, n_chunks, one_chunk, 0)

    @pl.when(step == pl.num_programs(2) - 1)
    def _():
        for hh in range(n_heads):
            sout_ref[hh] = st_ref[hh].T


def _hgrn(gates, lb, norm_w, s0, l_valid, block_len, chunk, heads_per_step):
    bsz, seq, width = gates.shape
    n_heads = width // (4 * HEAD_W)
    hb = heads_per_step
    n_grp = n_heads // hb

    def gate_spec(g):
        return pl.BlockSpec((None, block_len, hb * HEAD_W), lambda b, h, s: (b, s, g * n_grp + h))

    head_vec = pl.BlockSpec((1, hb * HEAD_W), lambda b, h, s: (0, h))
    state_spec = pl.BlockSpec((None, hb, HEAD_W, HEAD_W), lambda b, h, s: (b, h, 0, 0))
    return pl.pallas_call(
        functools.partial(_hgrn_kernel, n_heads=hb, chunk=chunk, n_chunks=block_len // chunk, l_valid=l_valid),
        grid=(bsz, n_grp, seq // block_len),
        in_specs=[gate_spec(0), gate_spec(1), gate_spec(2), gate_spec(3), head_vec,
                  pl.BlockSpec((1, HEAD_W), lambda b, h, s: (0, 0)), state_spec],
        out_specs=[pl.BlockSpec((None, block_len, hb * HEAD_W), lambda b, h, s: (b, s, h)), state_spec],
        out_shape=[jax.ShapeDtypeStruct((bsz, seq, n_heads * HEAD_W), BF16),
                   jax.ShapeDtypeStruct(s0.shape, F32)],
        scratch_shapes=[pltpu.VMEM((hb, HEAD_W, HEAD_W), F32)] + [pltpu.VMEM((hb, chunk, HEAD_W), F32)] * 3,
        compiler_params=_cparams("parallel", "parallel", "arbitrary"),
        name="hgrn",
    )(gates, gates, gates, gates, lb.reshape(1, -1), norm_w.reshape(1, HEAD_W), s0)


def _flash_kernel(q_ref, k_ref, v_ref, lam_ref, nw_ref, o_ref, qs_ref, sa_ref, sb_ref, m_ref, acc_ref,
                  *, tq, n_heads, post_scale):
    qi = pl.program_id(2)
    n_rows = 2 * tq
    lane = lax.broadcasted_iota(jnp.int32, (tq, HEAD_W), 1)
    zero = jnp.zeros((tq, HEAD_W), BF16)
    for hh in range(n_heads):
        q = q_ref[:, hh * HEAD_W:(hh + 1) * HEAD_W]
        qs_ref[hh, 0:tq, :] = jnp.where(lane < DA_HEAD_DIM, q, zero)
        qs_ref[hh, tq:n_rows, :] = jnp.where(lane >= DA_HEAD_DIM, q, zero)
    m_ref[...] = jnp.full(m_ref.shape, -jnp.inf, F32)
    acc_ref[...] = jnp.zeros(acc_ref.shape, F32)
    lane_reps = tq // HEAD_W
    ones_blk = jnp.ones((tq, HEAD_W), BF16)

    def tile_rows(t):
        return pl.ds(pl.multiple_of(t * tq, tq), tq)

    def scores(t, s_ref):
        for hh in range(n_heads):
            s_ref[hh] = _dot_nt(qs_ref[hh], k_ref[tile_rows(t), hh * HEAD_W:(hh + 1) * HEAD_W])

    def consume(t, s_ref, masked):
        for hh in range(n_heads):
            if masked:
                r = lax.broadcasted_iota(jnp.int32, (n_rows, tq), 0)
                c = lax.broadcasted_iota(jnp.int32, (n_rows, tq), 1)
                s_ref[hh] = jnp.where(c <= jnp.where(r >= tq, r - tq, r), s_ref[hh], -jnp.inf)
            part = s_ref[hh, :, 0:HEAD_W]
            for j in range(1, lane_reps):
                part = jnp.maximum(part, s_ref[hh, :, j * HEAD_W:(j + 1) * HEAD_W])
            m_old = m_ref[hh]
            m_new = jnp.maximum(m_old, jnp.max(part, axis=1, keepdims=True))
            alpha = jnp.exp(m_old - m_new)
            m_ref[hh] = m_new
            p = jnp.exp(s_ref[hh] - jnp.concatenate([m_new] * lane_reps, axis=1)).astype(BF16)
            v = v_ref[tile_rows(t), hh * HEAD_W:(hh + 1) * HEAD_W]
            pv = _dot(p, jnp.concatenate([v, ones_blk], axis=1))
            acc_ref[hh] = jnp.concatenate([alpha, alpha], axis=1) * acc_ref[hh] + pv

    scores(0, sa_ref)

    def pair(j, carry):
        t = 2 * j
        scores(t + 1, sb_ref)
        consume(t, sa_ref, False)
        scores(t + 2, sa_ref)
        consume(t + 1, sb_ref, False)
        return carry

    lax.fori_loop(0, lax.shift_right_logical(qi, 1), pair, 0)
    odd = lax.bitwise_and(qi, 1)

    @pl.when(odd == 1)
    def _():
        scores(qi, sb_ref)
        consume(qi - 1, sa_ref, False)
        consume(qi, sb_ref, True)

    @pl.when(odd == 0)
    def _():
        consume(qi, sa_ref, True)

    for hh in range(n_heads):
        out = acc_ref[hh, :, 0:HEAD_W] / acc_ref[hh, :, HEAD_W:2 * HEAD_W]
        ob = out[0:tq, :] - lam_ref[...] * out[tq:n_rows, :]
        o_ref[:, hh * HEAD_W:(hh + 1) * HEAD_W] = (_rms(ob, nw_ref[...]) * post_scale).astype(o_ref.dtype)


def _flash_attention(q, k, v, lam_vec, norm_w, post_scale, tq, heads_per_step):
    bsz, seq, width = q.shape
    hb = heads_per_step
    kv_spec = pl.BlockSpec((None, seq, hb * HEAD_W), lambda b, h, i: (b, 0, h))
    q_spec = pl.BlockSpec((None, tq, hb * HEAD_W), lambda b, h, i: (b, i, h))
    vec = pl.BlockSpec((1, HEAD_W), lambda b, h, i: (0, 0))
    return pl.pallas_call(
        functools.partial(_flash_kernel, tq=tq, n_heads=hb, post_scale=post_scale),
        grid=(bsz, width // (hb * HEAD_W), seq // tq),
        in_specs=[q_spec, kv_spec, kv_spec, vec, vec],
        out_specs=q_spec,
        out_shape=jax.ShapeDtypeStruct((bsz, seq, width), BF16),
        scratch_shapes=[pltpu.VMEM((hb, 2 * tq, HEAD_W), BF16), pltpu.VMEM((hb, 2 * tq, tq), F32),
                        pltpu.VMEM((hb, 2 * tq, tq), F32), pltpu.VMEM((hb, 2 * tq, HEAD_W), F32),
                        pltpu.VMEM((hb, 2 * tq, 2 * HEAD_W), F32)],
        compiler_params=_cparams("parallel", "parallel", "arbitrary"),
        name="flash_attention",
    )(q, k, v, lam_vec, norm_w.reshape(1, HEAD_W))


def _lane_to_col(vec):
    n = vec.shape[1]
    r = lax.broadcasted_iota(jnp.int32, (n, n), 0)
    c = lax.broadcasted_iota(jnp.int32, (n, n), 1)
    return jnp.sum(jnp.where(r == c, jnp.broadcast_to(vec, (n, n)), 0.0), axis=1, keepdims=True)


def _paged_kernel(pt_ref, q_ref, k_hbm, v_hbm, knew_ref, vnew_ref, lam_ref, nw_ref, o_ref,
                  wt_ref, m_ref, l_ref, acc_ref, kbuf, vbuf, sem, *, n_pp, n_heads, l_new, n_slots, post_scale):
    n_g = pl.num_programs(1)
    g = pl.program_id(1)
    step = pl.program_id(0) * n_g + g
    n_steps = pl.num_programs(0) * n_g

    def page_copies(s, slot):
        sb = lax.div(s, n_g)
        first = (s - sb * n_g) * n_pp
        copies = []
        for i in range(n_pp):
            page = pt_ref[sb, first + i]
            copies.append(pltpu.make_async_copy(k_hbm.at[page], kbuf.at[slot, i], sem.at[slot, 0]))
            copies.append(pltpu.make_async_copy(v_hbm.at[page], vbuf.at[slot, i], sem.at[slot, 1]))
        return copies

    @pl.when(step == 0)
    def _():
        for s in range(n_slots - 1):
            for cp in page_copies(s, s):
                cp.start()

    ahead = step + (n_slots - 1)

    @pl.when(ahead < n_steps)
    def _():
        for cp in page_copies(ahead, lax.rem(ahead, n_slots)):
            cp.start()

    @pl.when(g == 0)
    def _():
        m_ref[...] = jnp.full(m_ref.shape, -jnp.inf, F32)
        l_ref[...] = jnp.zeros(l_ref.shape, F32)
        acc_ref[...] = jnp.zeros(acc_ref.shape, F32)
        q_rows = jnp.concatenate([q_ref[...].astype(F32)] * (SUBLANES // l_new), axis=0)
        q_rep = jnp.concatenate([q_rows] * (HEAD_W // SUBLANES), axis=0)
        col = lax.broadcasted_iota(jnp.int32, q_rep.shape, 0)
        lane = lax.broadcasted_iota(jnp.int32, q_rep.shape, 1)
        own = (lax.div(lane, DA_HEAD_DIM) == lax.div(col, l_new)) & (col < n_heads * 2 * l_new)
        wt_ref[...] = jnp.where(own, q_rep, 0.0).astype(BF16)

    slot = lax.rem(step, n_slots)
    for cp in page_copies(step, slot):
        cp.wait()
    w_t = wt_ref[...]

    def softmax_step(s_t):
        m_old = m_ref[...]
        m_new = jnp.maximum(m_old, jnp.max(s_t, axis=0, keepdims=True))
        alpha = jnp.exp(m_old - m_new)
        p = jnp.exp(s_t - m_new)
        l_ref[...] = alpha * l_ref[...] + jnp.sum(p, axis=0, keepdims=True)
        m_ref[...] = m_new
        return _lane_to_col(alpha), p

    def page_rows(ref):
        heads = [ref[pl.ds(h, PAGE_SIZE, stride=n_heads), :].astype(BF16) for h in range(n_heads)]
        return jnp.concatenate(heads, axis=1)

    s_pages = [_dot_nt(page_rows(kbuf.at[slot, i]), w_t) for i in range(n_pp)]
    v_pages = [page_rows(vbuf.at[slot, i]) for i in range(n_pp)]
    alpha_col, p = softmax_step(jnp.concatenate(s_pages, axis=0))
    acc_ref[...] = alpha_col * acc_ref[...] + _dot(p.T.astype(BF16), jnp.concatenate(v_pages, axis=0))

    @pl.when(g == pl.num_programs(1) - 1)
    def _():
        s_new = _dot_nt(knew_ref[...].astype(BF16), w_t)
        s_row = lax.broadcasted_iota(jnp.int32, s_new.shape, 0)
        t_col = lax.rem(lax.broadcasted_iota(jnp.int32, s_new.shape, 1), l_new)
        alpha_col, p = softmax_step(jnp.where(s_row <= t_col, s_new, -jnp.inf))
        acc = alpha_col * acc_ref[...]
        for s in range(l_new):
            acc = acc + _lane_to_col(p[s:s + 1, :]) * vnew_ref[s:s + 1, :]
        acc_ref[...] = acc
        inv_l = 1.0 / _lane_to_col(l_ref[...])
        lam = lam_ref[...]
        for h in range(n_heads):
            rs = slice(h * 2 * l_new, (h + 1) * 2 * l_new)
            cs = slice(h * HEAD_W, (h + 1) * HEAD_W)
            maps = acc_ref[rs, cs] * inv_l[rs, :]
            ob = maps - lam * pltpu.roll(maps, l_new, 0)
            o_ref[:, cs] = _rms(ob, nw_ref[...]) * post_scale


def _paged_attention(q, k_new, v_new, cache_k, cache_v, page_table, lam_vec, norm_w, post_scale, n_pp):
    bsz, l_new, width = q.shape
    n_heads = width // HEAD_W
    n_pages = page_table.shape[1]
    assert 2 * l_new == SUBLANES and n_heads * 2 * l_new <= HEAD_W and n_pages % n_pp == 0
    pad_rows = ((0, 0), (0, SUBLANES - l_new), (0, 0))
    k_new, v_new = jnp.pad(k_new, pad_rows), jnp.pad(v_new, pad_rows)
    n_slots = 3
    page_shape = (PAGE_SIZE * n_heads, HEAD_W)
    new_spec = pl.BlockSpec((None, SUBLANES, width), lambda b, g, pt: (b, 0, 0))
    vec = pl.BlockSpec((1, HEAD_W), lambda b, g, pt: (0, 0))
    hbm = pl.BlockSpec(memory_space=pl.ANY)
    out = pl.pallas_call(
        functools.partial(_paged_kernel, n_pp=n_pp, n_heads=n_heads, l_new=l_new, n_slots=n_slots,
                          post_scale=post_scale),
        grid_spec=pltpu.PrefetchScalarGridSpec(
            num_scalar_prefetch=1,
            grid=(bsz, n_pages // n_pp),
            in_specs=[pl.BlockSpec((None, l_new, width), lambda b, g, pt: (b, 0, 0)), hbm, hbm,
                      new_spec, new_spec, vec, vec],
            out_specs=new_spec,
            scratch_shapes=[pltpu.VMEM((HEAD_W, width), BF16), pltpu.VMEM((1, HEAD_W), F32),
                            pltpu.VMEM((1, HEAD_W), F32), pltpu.VMEM((HEAD_W, width), F32),
                            pltpu.VMEM((n_slots, n_pp) + page_shape, F32),
                            pltpu.VMEM((n_slots, n_pp) + page_shape, F32),
                            pltpu.SemaphoreType.DMA((n_slots, 2))],
        ),
        out_shape=jax.ShapeDtypeStruct((bsz, SUBLANES, width), F32),
        compiler_params=_cparams("arbitrary", "arbitrary"),
        name="paged_attention",
    )(page_table, q, cache_k, cache_v, k_new, v_new, lam_vec, norm_w.reshape(1, HEAD_W))
    return out[:, :l_new, :].astype(BF16)


def _merge_kernel(oa_ref, ob_ref, wa_ref, wb_ref, ga_ref, gb_ref, u_ref, wa_bf, wb_bf):
    @pl.when(pl.program_id(1) == 0)
    def _():
        wa_bf[...] = wa_ref[...].astype(BF16)
        wb_bf[...] = wb_ref[...].astype(BF16)

    u = ga_ref[...] * _dot(oa_ref[...], wa_bf[...]) + gb_ref[...] * _dot(ob_ref[...], wb_bf[...])
    u_ref[...] = u.astype(u_ref.dtype)


def _merge(oa, ob, w_a, w_b, sig_gates, tm, tn):
    t, k = oa.shape
    n = w_a.shape[1]
    nj = n // tn
    row = pl.BlockSpec((tm, k), lambda j, i: (i, 0))
    wsp = pl.BlockSpec((k, tn), lambda j, i: (0, j))
    return pl.pallas_call(
        _merge_kernel,
        grid=(nj, t // tm),
        in_specs=[row, row, wsp, wsp, pl.BlockSpec((tm, tn), lambda j, i: (i, j)),
                  pl.BlockSpec((tm, tn), lambda j, i: (i, j + nj))],
        out_specs=pl.BlockSpec((tm, tn), lambda j, i: (i, j)),
        out_shape=jax.ShapeDtypeStruct((t, n), BF16),
        scratch_shapes=[pltpu.VMEM((k, tn), BF16), pltpu.VMEM((k, tn), BF16)],
        compiler_params=_cparams("arbitrary", "arbitrary"),
        name="branch_merge",
    )(oa, ob, w_a, w_b, sig_gates, sig_gates)


def _outproj_kernel(u_ref, w_ref, x_ref, nw_post_ref, nw_pre_ref, x1_ref, h2_ref):
    y = _dot(u_ref[...], w_ref[...])
    x1 = x_ref[...] + _rms(y, nw_post_ref[...])
    x1_ref[...] = x1
    h2_ref[...] = _rms(x1, nw_pre_ref[...]).astype(h2_ref.dtype)


def _outproj(u, w_out_bf, x, nw_post, nw_pre, tm):
    t, d = x.shape
    row = pl.BlockSpec((tm, d), lambda i: (i, 0))
    vec = pl.BlockSpec((1, d), lambda i: (0, 0))
    return pl.pallas_call(
        _outproj_kernel,
        grid=(t // tm,),
        in_specs=[row, pl.BlockSpec((d, d), lambda i: (0, 0)), row, vec, vec],
        out_specs=[row, row],
        out_shape=[jax.ShapeDtypeStruct((t, d), F32), jax.ShapeDtypeStruct((t, d), BF16)],
        compiler_params=_cparams("parallel"),
        name="out_proj",
    )(u, w_out_bf, x, nw_post.reshape(1, d), nw_pre.reshape(1, d))


def _gelu_tanh(x):
    return 0.5 * x * (1.0 + jnp.tanh(math.sqrt(2.0 / math.pi) * (x + 0.044715 * (x * x * x))))


def _ffn_up_kernel(*refs, seq_len, tiles_per_seq):
    if tiles_per_seq:
        h_ref, wg_ref, wu_ref, cw_ref, cb_ref, prev_ref, ff_ref, tail_ref, wg_bf, wu_bf, carry_ref = refs
    else:
        h_ref, wg_ref, wu_ref, cw_ref, cb_ref, halo1_ref, halo2_ref, ff_ref, g_ref, wg_bf, wu_bf = refs
    i = pl.program_id(1)

    @pl.when(i == 0)
    def _():
        wg_bf[...] = wg_ref[...].astype(BF16)
        wu_bf[...] = wu_ref[...].astype(BF16)

    if tiles_per_seq:
        @pl.when(i % tiles_per_seq == 0)
        def _():
            carry_ref[...] = prev_ref[...]

    h = h_ref[...]
    tm, tn = ff_ref.shape
    cn = min(tn, MXU_WIDTH)
    row = lax.broadcasted_iota(jnp.int32, (tm, cn), 0)
    for c in range(tn // cn):
        cs = slice(c * cn, (c + 1) * cn)
        g = _dot(h, wg_bf[:, cs])
        up = _dot(h, wu_bf[:, cs])
        g1 = pltpu.roll(g, 1, 0)
        g2 = pltpu.roll(g, 2, 0)
        if tiles_per_seq:
            p0, p1 = carry_ref[0:1, cs], carry_ref[1:2, cs]
            g1 = jnp.where(row == 0, p1, g1)
            g2 = jnp.where(row == 0, p0, jnp.where(row == 1, p1, g2))
            carry_ref[:, cs] = g[tm - 2:tm, :]
            tail_ref[:, cs] = g[tm - 2:tm, :]
        else:
            t_in_seq = lax.rem(row, seq_len)
            g1 = jnp.where(t_in_seq >= 1, g1, halo1_ref[:, cs])
            g2 = jnp.where(t_in_seq >= 2, g2, halo2_ref[:, cs])
            g_ref[:, cs] = g
        conv = cb_ref[:, cs] + g2 * cw_ref[0:1, cs] + g1 * cw_ref[1:2, cs] + g * cw_ref[2:3, cs]
        ff_ref[:, cs] = (_gelu_tanh(conv) * up).astype(ff_ref.dtype)


def _ffn_up(h2, w_gate, w_up, conv_w, conv_b, conv_prev, seq_len, tm, tn):
    t, d = h2.shape
    f = w_gate.shape[1]
    bsz = t // seq_len
    row = pl.BlockSpec((tm, d), lambda j, i: (i, 0))
    wsp = pl.BlockSpec((d, tn), lambda j, i: (0, j))
    cw = pl.BlockSpec((CONV_W, tn), lambda j, i: (0, j))
    cb = pl.BlockSpec((1, tn), lambda j, i: (0, j))
    tile = pl.BlockSpec((tm, tn), lambda j, i: (i, j))
    scratch = [pltpu.VMEM((d, tn), BF16), pltpu.VMEM((d, tn), BF16)]
    if seq_len % tm == 0:
        tps = seq_len // tm
        state = pl.BlockSpec((None, CONV_W - 1, tn), lambda j, i: (i // tps, 0, j))
        ff, tail = pl.pallas_call(
            functools.partial(_ffn_up_kernel, seq_len=seq_len, tiles_per_seq=tps),
            grid=(f // tn, t // tm),
            in_specs=[row, wsp, wsp, cw, cb, state],
            out_specs=[tile, state],
            out_shape=[jax.ShapeDtypeStruct((t, f), BF16), jax.ShapeDtypeStruct((bsz, CONV_W - 1, f), F32)],
            scratch_shapes=scratch + [pltpu.VMEM((CONV_W - 1, tn), F32)],
            compiler_params=_cparams("arbitrary", "arbitrary"),
            name="ffn_up_long",
        )(h2, w_gate, w_up, conv_w, conv_b.reshape(1, f), conv_prev)
        return ff, tail
    assert tm % seq_len == 0 and seq_len >= CONV_W - 1
    zeros = jnp.zeros((bsz, seq_len - 1, f), F32)
    halo1 = jnp.concatenate([conv_prev[:, 1:2], zeros], axis=1).reshape(t, f)
    halo2 = jnp.concatenate([conv_prev, zeros[:, 1:]], axis=1).reshape(t, f)
    ff, g = pl.pallas_call(
        functools.partial(_ffn_up_kernel, seq_len=seq_len, tiles_per_seq=0),
        grid=(f // tn, t // tm),
        in_specs=[row, wsp, wsp, cw, cb, tile, tile],
        out_specs=[tile, tile],
        out_shape=[jax.ShapeDtypeStruct((t, f), BF16), jax.ShapeDtypeStruct((t, f), F32)],
        scratch_shapes=scratch,
        compiler_params=_cparams("arbitrary", "arbitrary"),
        name="ffn_up_short",
    )(h2, w_gate, w_up, conv_w, conv_b.reshape(1, f), halo1, halo2)
    return ff, g.reshape(bsz, seq_len, f)[:, seq_len - (CONV_W - 1):, :]


def _ffn_down_kernel(ff_ref, w_ref, x_ref, nw_ref, y_ref):
    y_ref[...] = x_ref[...] + _rms(_dot(ff_ref[...], w_ref[...]), nw_ref[...])


def _ffn_down(ff, w_down_bf, x1, nw, tm):
    t, f = ff.shape
    d = x1.shape[1]
    row = pl.BlockSpec((tm, d), lambda i: (i, 0))
    return pl.pallas_call(
        _ffn_down_kernel,
        grid=(t // tm,),
        in_specs=[pl.BlockSpec((tm, f), lambda i: (i, 0)),
                  pl.BlockSpec((f, d), lambda i: (0, 0), pipeline_mode=pl.Buffered(1)),
                  row, pl.BlockSpec((1, d), lambda i: (0, 0))],
        out_specs=row,
        out_shape=jax.ShapeDtypeStruct((t, d), F32),
        compiler_params=_cparams("parallel"),
        name="ffn_down",
    )(ff, w_down_bf, x1, nw.reshape(1, d))


def _tile(n, pref):
    return pref if n % pref == 0 else n


def _layer(x, pos, attend, hg_s0, conv_prev, p, lb, layer_idx):
    bsz, seq, d = x.shape
    t = bsz * seq
    hg_w = p["w_branch_a"].shape[0]
    da_w = p["w_branch_b"].shape[0]
    x2 = x.reshape(t, d)
    tm = _tile(t, 1024)

    h = _norm(x2, p["norm_mix_pre"], _tile(t, 512))
    w_in = p["w_in"]
    gates = _in_proj(h, w_in, 0, 4 * hg_w, "plain", tm, 1024)[0]
    tabs = _rotary_tables(pos)
    if seq % tm:
        tabs = tuple(jnp.tile(tb, (tm // seq, 1)) for tb in tabs)
    q_bf = _in_proj(h, w_in, 4 * hg_w, da_w, "rot_q", tm, 1024, tabs)[0]
    k_rows, k_bf = _in_proj(h, w_in, 4 * hg_w + da_w, da_w, "rot_k", tm, 1024, tabs)
    v_rows, v_bf = _in_proj(h, w_in, 4 * hg_w + 2 * da_w, da_w, "v", tm, 1024)
    sig_gates = _in_proj(h, w_in, 4 * hg_w + 3 * da_w, 2 * d, "sigmoid", tm, 1024)[0]

    gates3 = gates.reshape(bsz, seq, 4 * hg_w)
    if seq % 64 == 0:
        oa, s_new = _hgrn(gates3, lb, p["hg_norm"], hg_s0, seq, _tile(seq, 512), 256, 8)
    else:
        pad = (-seq) % SUBLANES
        gates3 = jnp.pad(gates3, ((0, 0), (0, pad), (0, 0)))
        oa, s_new = _hgrn(gates3, lb, p["hg_norm"], hg_s0, seq, seq + pad, seq + pad, hg_w // HEAD_W)
        oa = oa[:, :seq, :]
    oa = oa.reshape(t, hg_w)

    lam_init = 0.8 - 0.6 * math.exp(-0.3 * layer_idx)
    lam = p["da_lambda"].astype(F32)
    lam_full = jnp.exp(jnp.sum(lam[0] * lam[1])) - jnp.exp(jnp.sum(lam[2] * lam[3])) + lam_init
    lam_vec = jnp.full((1, HEAD_W), lam_full, F32)
    shp = (bsz, seq, da_w)
    ob = attend(q_bf.reshape(shp), k_bf.reshape(shp), v_bf.reshape(shp), k_rows, v_rows,
                lam_vec, p["da_subln"], 1.0 - lam_init).reshape(t, da_w)

    u = _merge(oa, ob, p["w_branch_a"], p["w_branch_b"], sig_gates, tm, 1024)
    x1, h2 = _outproj(u, p["w_out"].astype(BF16), x2, p["norm_mix_post"], p["norm_ffn_pre"], _tile(t, 512))
    ff, conv_new = _ffn_up(h2, p["w_ffn_gate"], p["w_ffn_up"], p["conv_w"], p["conv_b"], conv_prev, seq, tm, 512)
    y = _ffn_down(ff, p["w_ffn_down"].astype(BF16), x1, p["norm_ffn_post"], _tile(t, 512))
    n_da_heads = da_w // HEAD_W
    return (y.reshape(bsz, seq, d), k_rows.reshape(bsz, seq, n_da_heads, HEAD_W),
            v_rows.reshape(bsz, seq, n_da_heads, HEAD_W), s_new, conv_new)


def kernel(x_prompt, x_sample, cache_k, cache_v, page_table, state_hgrn, state_conv, norm_mix_pre, w_in, hg_lb_logits, hg_norm, da_lambda, da_subln, w_branch_a, w_branch_b, w_out, norm_mix_post, norm_ffn_pre, w_ffn_gate, w_ffn_up, conv_w, conv_b, w_ffn_down, norm_ffn_post):
    depth = w_in.shape[0]
    lb_all = jnp.cumsum(jax.nn.softmax(hg_lb_logits.astype(F32), axis=0), axis=0)
    bp, lp = x_prompt.shape[:2]
    ls = x_sample.shape[1]
    past_len = page_table.shape[1] * PAGE_SIZE
    pos_p = jnp.arange(lp)
    pos_s = past_len + jnp.arange(ls)
    n_hg_heads = state_hgrn.shape[2]
    d_ff = state_conv.shape[-1]
    yp, ys = x_prompt, x_sample
    outs = [[] for _ in range(8)]
    for l in range(depth):
        p = dict(norm_mix_pre=norm_mix_pre[l], w_in=w_in[l], hg_norm=hg_norm[l], da_lambda=da_lambda[l],
                 da_subln=da_subln[l], w_branch_a=w_branch_a[l], w_branch_b=w_branch_b[l], w_out=w_out[l],
                 norm_mix_post=norm_mix_post[l], norm_ffn_pre=norm_ffn_pre[l], w_ffn_gate=w_ffn_gate[l],
                 w_ffn_up=w_ffn_up[l], conv_w=conv_w[l], conv_b=conv_b[l], w_ffn_down=w_ffn_down[l],
                 norm_ffn_post=norm_ffn_post[l])

        def attend_prompt(q, k, v, k_rows, v_rows, lam_vec, norm_w, post_scale):
            return _flash_attention(q, k, v, lam_vec, norm_w, post_scale, _tile(q.shape[1], 512), 1)

        def attend_sample(q, k, v, k_rows, v_rows, lam_vec, norm_w, post_scale, l=l):
            width = q.shape[-1]
            rows_per_page = PAGE_SIZE * (width // HEAD_W)
            ck = cache_k[l].reshape(cache_k.shape[1], rows_per_page, HEAD_W)
            cv = cache_v[l].reshape(cache_v.shape[1], rows_per_page, HEAD_W)
            return _paged_attention(q, k_rows.reshape(q.shape), v_rows.reshape(q.shape), ck, cv, page_table,
                                    lam_vec, norm_w, post_scale,
                                    _tile(page_table.shape[1], 8))

        yp, kp, vp, sp, cp = _layer(
            yp, pos_p, attend_prompt, jnp.zeros((bp, n_hg_heads, HEAD_W, HEAD_W), F32),
            jnp.zeros((bp, CONV_W - 1, d_ff), F32), p, lb_all[l], l)
        ys, kss, vss, sss, css = _layer(ys, pos_s, attend_sample, state_hgrn[l], state_conv[l], p, lb_all[l], l)
        for lst, val in zip(outs, (kp, vp, sp, cp, kss, vss, sss, css)):
            lst.append(val)
    kp, vp, sp, cp, kss, vss, sss, css = (jnp.stack(o) for o in outs)
    return (yp, ys, kp, vp, sp.astype(state_hgrn.dtype), cp, kss, vss, sss.astype(state_hgrn.dtype), css)
```
